```python
import math
import jax
import jax.numpy as jnp
from jax import lax
import numpy as np

D_MODEL = 1024
BATCH = 8
SEQ = 4096
DEPTH = 4

GRID_W = 64
CTX_LEN = 256
EPS = 1e-6

BRANCH_W = 512
N_BRANCH = 3

ATT_HEADS = 4
ATT_QK_DIM = 64
ATT_V_DIM = 2 * ATT_QK_DIM
ATT_WIDTH = ATT_HEADS * ATT_V_DIM
ROPE_BASE = 10000.0
Q_BLOCK = 128

LRU_WIDTH = BRANCH_W
LRU_BLOCKS = 8
LRU_BLOCK = LRU_WIDTH // LRU_BLOCKS
LRU_C = 8.0
CONV_W = 4
CONV_PAD = (2, 1)

S5_WIDTH = BRANCH_W
S5_GROUP = 16
S5_GROUPS = S5_WIDTH // S5_GROUP
S5_STATE = 64

SPLITS = (ATT_HEADS * 2 * ATT_QK_DIM, ATT_HEADS * 2 * ATT_QK_DIM, ATT_WIDTH, ATT_WIDTH,
          LRU_WIDTH, LRU_WIDTH, S5_WIDTH, S5_WIDTH, N_BRANCH * D_MODEL)
IN_DIM = sum(SPLITS)

kernel_name = 'hybrid_diffattn_rglru_s5_prefix_block'


def _split_points(sizes):
    pts, acc = [], 0
    for s in sizes[:-1]:
        acc += s
        pts.append(acc)
    return pts


def rmsnorm(x, g):
    xf = x.astype(jnp.float32)
    y = xf * lax.rsqrt(jnp.mean(jnp.square(xf), axis=-1, keepdims=True) + EPS)
    return (y * g.astype(jnp.float32)).astype(x.dtype)


def ada_mod(cond, w_mod, b_mod):
    m = jax.nn.silu(cond) @ w_mod + b_mod
    return jnp.split(m, 3, axis=-1)


def axial_rope(n_tokens):
    rows = n_tokens // GRID_W
    r = jnp.repeat(jnp.arange(rows, dtype=jnp.float32), GRID_W)
    col = jnp.tile(jnp.arange(GRID_W, dtype=jnp.float32), rows)
    n_freq = ATT_QK_DIM // 4
    inv = ROPE_BASE ** (-jnp.arange(n_freq, dtype=jnp.float32) / n_freq)
    ang = jnp.concatenate([r[:, None] * inv, col[:, None] * inv], axis=-1)
    return jnp.cos(ang), jnp.sin(ang)


def apply_rope(x, cos, sin):
    cos = cos[:, None, None, :]
    sin = sin[:, None, None, :]
    x1 = x[..., 0::2].astype(jnp.float32)
    x2 = x[..., 1::2].astype(jnp.float32)
    out = jnp.stack([x1 * cos - x2 * sin, x1 * sin + x2 * cos], axis=-1).reshape(x.shape)
    return out.astype(x.dtype)


def diff_attend(q, k, v, lam):
    s = jnp.einsum('bqhcd,bkhcd->bhcqk', q, k, preferred_element_type=jnp.float32) * (ATT_QK_DIM ** -0.5)
    p = jax.nn.softmax(s, axis=-1)
    w = p[:, :, 0] - lam * p[:, :, 1]
    return jnp.einsum('bhqk,bkhd->bqhd', w.astype(v.dtype), v)


def diff_attention_branch(q_l, k_l, v_l, q_c, k_c, v_c, lam_qk, subln_g, lam_init, need_ctx):
    b, t = q_l.shape[:2]
    lq = lam_qk.astype(jnp.float32)
    lam = jnp.exp(jnp.sum(lq[0] * lq[1])) - jnp.exp(jnp.sum(lq[2] * lq[3])) + lam_init
    cos, sin = axial_rope(t)
    q_l = apply_rope(q_l, cos, sin)
    k_l = apply_rope(k_l, cos, sin)
    k_all = jnp.concatenate([k_l, k_c], axis=1)
    v_all = jnp.concatenate([v_l, v_c], axis=1)
    n_blk = t // Q_BLOCK
    q_blocks = jnp.moveaxis(q_l.reshape(b, n_blk, Q_BLOCK, *q_l.shape[2:]), 1, 0)
    o = lax.map(lambda qb: diff_attend(qb, k_all, v_all, lam), q_blocks)
    o_l = jnp.moveaxis(o, 0, 1).reshape(b, t, ATT_HEADS, ATT_V_DIM)

    def post(o_):
        return (rmsnorm(o_, subln_g) * (1.0 - lam_init)).reshape(*o_.shape[:2], ATT_WIDTH)

    y_c = post(diff_attend(q_c, k_c, v_c, lam)) if need_ctx else None
    return post(o_l), y_c


def depthwise_conv(x, w, b):
    y = lax.conv_general_dilated(x, w[:, None, :].astype(x.dtype), (1,), [CONV_PAD],
                                 dimension_numbers=('NWC', 'WIO', 'NWC'),
                                 feature_group_count=x.shape[-1])
    return y + b


def block_diag(x, w, b):
    xb = x.reshape(*x.shape[:-1], LRU_BLOCKS, LRU_BLOCK)
    return jnp.einsum('btnc,ncd->btnd', xb, w).reshape(x.shape) + b


def rglru_coeffs(x, w_a, b_a, w_x, b_x, lam):
    r = jax.nn.sigmoid(block_diag(x, w_a, b_a).astype(jnp.float32))
    i = jax.nn.sigmoid(block_diag(x, w_x, b_x).astype(jnp.float32))
    log_a = -LRU_C * r * jax.nn.softplus(-lam.astype(jnp.float32))
    a = jnp.exp(log_a)
    bterm = jnp.sqrt(-jnp.expm1(2.0 * log_a)) * (i * x.astype(jnp.float32))
    return a, bterm


def _lin_combine(e1, e2):
    a1, b1 = e1
    a2, b2 = e2
    return a1 * a2, a2 * b1 + b2


def linear_scan(a, b, h0, reverse):
    if h0 is not None:
        first = -1 if reverse else 0
        b = b.at[:, first].add(a[:, first] * h0)
    return lax.associative_scan(_lin_combine, (a, b), axis=1, reverse=reverse)[1]


def rglru_branch(x_l, x_c, conv_w, conv_b, w_a, b_a, w_x, b_x, lam):
    u_l = depthwise_conv(x_l, conv_w, conv_b)
    u_c = depthwise_conv(x_c, conv_w, conv_b)
    hs_l, hs_c = [], []
    for d, rev in enumerate((False, True)):
        a_c, b_c = rglru_coeffs(u_c, w_a[d], b_a[d], w_x[d], b_x[d], lam[d])
        h_c = linear_scan(a_c, b_c, None, rev)
        h0 = h_c[:, 0] if rev else h_c[:, -1]
        a_l, b_l = rglru_coeffs(u_l, w_a[d], b_a[d], w_x[d], b_x[d], lam[d])
        hs_l.append(linear_scan(a_l, b_l, h0, rev))
        hs_c.append(h_c)
    return (hs_l[0] + hs_l[1]).astype(x_l.dtype), (hs_c[0] + hs_c[1]).astype(x_c.dtype)


def s5_discretise(lam_re, lam_im, log_dt, b_re, b_im):
    dt = jnp.exp(log_dt.astype(jnp.float32))[:, None]
    lr = lam_re.astype(jnp.float32)
    li = lam_im.astype(jnp.float32)
    mag = jnp.exp(lr * dt)
    ar = mag * jnp.cos(li * dt)
    ai = mag * jnp.sin(li * dt)
    den = lr * lr + li * li
    nr = ar - 1.0
    cr = (nr * lr + ai * li) / den
    ci = (ai * lr - nr * li) / den
    br = cr[..., None] * b_re - ci[..., None] * b_im
    bi = cr[..., None] * b_im + ci[..., None] * b_re
    return ar, ai, br, bi


def _cplx_combine(e1, e2):
    ar1, ai1, br1, bi1 = e1
    ar2, ai2, br2, bi2 = e2
    return (ar1 * ar2 - ai1 * ai2, ar1 * ai2 + ai1 * ar2,
            ar2 * br1 - ai2 * bi1 + br2, ar2 * bi1 + ai2 * br1 + bi2)


def complex_scan(ar, ai, br, bi, h0, reverse):
    if h0 is not None:
        h0r, h0i = h0
        first = -1 if reverse else 0
        br = br.at[:, first].add(ar * h0r - ai * h0i)
        bi = bi.at[:, first].add(ar * h0i + ai * h0r)
    a_r = jnp.broadcast_to(ar, br.shape)
    a_i = jnp.broadcast_to(ai, br.shape)
    _, _, hr, hi = lax.associative_scan(_cplx_combine, (a_r, a_i, br, bi), axis=1, reverse=reverse)
    return hr, hi


def s5_branch(u_l, u_c, lam_re, lam_im, log_dt, b_re, b_im, c_re, c_im, d_skip, w_glu, b_glu, need_ctx):
    def drive(u, bbr, bbi):
        ug = u.reshape(*u.shape[:2], S5_GROUPS, S5_GROUP).astype(jnp.float32)
        return jnp.einsum('btgh,gph->btgp', ug, bbr), jnp.einsum('btgh,gph->btgp', ug, bbi)

    def readout(u, hr, hi):
        y = jnp.einsum('btgp,ghp->btgh', hr, c_re) - jnp.einsum('btgp,ghp->btgh', hi, c_im)
        y = y.reshape(u.shape).astype(u.dtype) + d_skip * u
        g = jax.nn.gelu(y)
        return g * jax.nn.sigmoid(g @ w_glu + b_glu)

    hl_r, hl_i, hc_r, hc_i = [], [], [], []
    for d, rev in enumerate((False, True)):
        ar, ai, bbr, bbi = s5_discretise(lam_re[d], lam_im[d], log_dt[d], b_re[d], b_im[d])
        cr_, ci_ = complex_scan(ar, ai, *drive(u_c, bbr, bbi), None, rev)
        idx = 0 if rev else -1
        lr_, li_ = complex_scan(ar, ai, *drive(u_l, bbr, bbi), (cr_[:, idx], ci_[:, idx]), rev)
        hl_r.append(lr_)
        hl_i.append(li_)
        hc_r.append(cr_)
        hc_i.append(ci_)
    y_l = readout(u_l, hl_r[0] + hl_r[1], hl_i[0] + hl_i[1])
    y_c = readout(u_c, hc_r[0] + hc_r[1], hc_i[0] + hc_i[1]) if need_ctx else None
    return y_l, y_c


def gated_merge(ya, yr, ys, za, zr, zs, gl, w_branch, w_out):
    y = jnp.stack([ya * jax.nn.silu(za), yr * jax.nn.silu(zr), ys * jax.nn.silu(zs)], axis=2)
    yb = jnp.einsum('btnw,nwd->btnd', y, w_branch)
    g = jax.nn.sigmoid(gl.reshape(*gl.shape[:2], N_BRANCH, D_MODEL))
    return jnp.sum(g * yb, axis=2) @ w_out


def _qk_heads(t):
    return t.reshape(*t.shape[:2], ATT_HEADS, 2, ATT_QK_DIM)


def _v_heads(t):
    return t.reshape(*t.shape[:2], ATT_HEADS, ATT_V_DIM)


def hybrid_layer(x, ctx, c, c_ctx, p, layer_idx, need_ctx):
    lam_init = 0.8 - 0.6 * math.exp(-0.3 * layer_idx)
    sh_x, sc_x, gt_x = ada_mod(c, p['w_mod'], p['b_mod'])
    sh_c, sc_c, gt_c = ada_mod(c_ctx, p['w_mod'], p['b_mod'])
    h_l = rmsnorm(x, p['norm_g']) * (1.0 + sc_x[:, None]) + sh_x[:, None]
    h_c = rmsnorm(ctx, p['norm_g']) * (1.0 + sc_c) + sh_c
    pts = _split_points(SPLITS)
    q_l, k_l, v_l, za_l, xr_l, zr_l, us_l, zs_l, gl_l = jnp.split(h_l @ p['w_in'], pts, axis=-1)
    q_c, k_c, v_c, za_c, xr_c, zr_c, us_c, zs_c, gl_c = jnp.split(h_c @ p['w_in'], pts, axis=-1)

    ya_l, ya_c = diff_attention_branch(_qk_heads(q_l), _qk_heads(k_l), _v_heads(v_l),
                                       _qk_heads(q_c), _qk_heads(k_c), _v_heads(v_c),
                                       p['lam_qk'], p['subln_g'], lam_init, need_ctx)
    yr_l, yr_c = rglru_branch(xr_l, xr_c, p['conv_w'], p['conv_b'], p['lru_wa'], p['lru_ba'],
                              p['lru_wx'], p['lru_bx'], p['lru_lam'])
    ys_l, ys_c = s5_branch(us_l, us_c, p['s5_lam_re'], p['s5_lam_im'], p['s5_log_dt'], p['s5_b_re'],
                           p['s5_b_im'], p['s5_c_re'], p['s5_c_im'], p['s5_d'], p['s5_w_glu'],
                           p['s5_b_glu'], need_ctx)

    x = x + gt_x[:, None] * gated_merge(ya_l, yr_l, ys_l, za_l, zr_l, zs_l, gl_l, p['w_branch'], p['w_out'])
    if need_ctx:
        ctx = ctx + gt_c * gated_merge(ya_c, yr_c, ys_c, za_c, zr_c, zs_c, gl_c, p['w_branch'], p['w_out'])
    return x, ctx


def setup_inputs(seed: int = 0) -> dict:
    key = jax.random.key(seed)
    ks = jax.random.split(key, 32)
    f32 = jnp.float32
    L = DEPTH

    def nrm(k, shape, s):
        return jax.random.normal(k, shape, f32) * s

    a_c = jax.random.uniform(ks[16], (L, 2, LRU_WIDTH), f32, minval=0.9, maxval=0.999)
    a_base = a_c ** (1.0 / LRU_C)
    return {
        'x': nrm(ks[0], (BATCH, SEQ, D_MODEL), 1.0),
        'c': nrm(ks[1], (BATCH, D_MODEL), 1.0),
        'ctx': nrm(ks[2], (BATCH, CTX_LEN, D_MODEL), 1.0),
        'c_ctx': nrm(ks[3], (D_MODEL,), 1.0),
        'w_mod': nrm(ks[4], (L, D_MODEL, 3 * D_MODEL), 0.5 * D_MODEL ** -0.5),
        'b_mod': nrm(ks[5], (L, 3 * D_MODEL), 0.01),
        'norm_g': 1.0 + nrm(ks[6], (L, D_MODEL), 0.01),
        'w_in': nrm(ks[7], (L, D_MODEL, IN_DIM), D_MODEL ** -0.5),
        'lam_qk': nrm(ks[8], (L, 4, ATT_QK_DIM), 0.1),
        'subln_g': 1.0 + nrm(ks[9], (L, ATT_V_DIM), 0.01),
        'conv_w': nrm(ks[10], (L, CONV_W, LRU_WIDTH), CONV_W ** -0.5),
        'conv_b': nrm(ks[11], (L, LRU_WIDTH), 0.01),
        'lru_wa': nrm(ks[12], (L, 2, LRU_BLOCKS, LRU_BLOCK, LRU_BLOCK), LRU_BLOCK ** -0.5),
        'lru_ba': nrm(ks[13], (L, 2, LRU_WIDTH), 0.01),
        'lru_wx': nrm(ks[14], (L, 2, LRU_BLOCKS, LRU_BLOCK, LRU_BLOCK), LRU_BLOCK ** -0.5),
        'lru_bx': nrm(ks[15], (L, 2, LRU_WIDTH), 0.01),
        'lru_lam': jnp.log(a_base) - jnp.log1p(-a_base),
        's5_lam_re': -0.5 + nrm(ks[17], (L, 2, S5_GROUPS, S5_STATE), 0.01),
        's5_lam_im': jnp.pi * jnp.arange(S5_STATE, dtype=f32) + nrm(ks[18], (L, 2, S5_GROUPS, S5_STATE), 0.01),
        's5_log_dt': jax.random.uniform(ks[19], (L, 2, S5_GROUPS), f32,
                                        minval=math.log(1e-3), maxval=math.log(1e-1)),
        's5_b_re': nrm(ks[20], (L, 2, S5_GROUPS, S5_STATE, S5_GROUP), (2 * S5_GROUP) ** -0.5),
        's5_b_im': nrm(ks[21], (L, 2, S5_GROUPS, S5_STATE, S5_GROUP), (2 * S5_GROUP) ** -0.5),
        's5_c_re': nrm(ks[22], (L, S5_GROUPS, S5_GROUP, S5_STATE), (2 * S5_STATE) ** -0.5),
        's5_c_im': nrm(ks[23], (L, S5_GROUPS, S5_GROUP, S5_STATE), (2 * S5_STATE) ** -0.5),
        's5_d': nrm(ks[24], (L, S5_WIDTH), 1.0),
        's5_w_glu': nrm(ks[25], (L, S5_WIDTH, S5_WIDTH), S5_WIDTH ** -0.5),
        's5_b_glu': nrm(ks[26], (L, S5_WIDTH), 0.01),
        'w_branch': nrm(ks[27], (L, N_BRANCH, BRANCH_W, D_MODEL), BRANCH_W ** -0.5),
        'w_out': nrm(ks[28], (L, D_MODEL, D_MODEL), D_MODEL ** -0.5),
        'final_g': 1.0 + nrm(ks[29], (D_MODEL,), 0.01),
    }


def reference(x, c, ctx, c_ctx, w_mod, b_mod, norm_g, w_in, lam_qk, subln_g, conv_w, conv_b,
              lru_wa, lru_ba, lru_wx, lru_bx, lru_lam, s5_lam_re, s5_lam_im, s5_log_dt,
              s5_b_re, s5_b_im, s5_c_re, s5_c_im, s5_d, s5_w_glu, s5_b_glu, w_branch, w_out, final_g):
    for l in range(DEPTH):
        p = dict(w_mod=w_mod[l], b_mod=b_mod[l], norm_g=norm_g[l], w_in=w_in[l], lam_qk=lam_qk[l],
                 subln_g=subln_g[l], conv_w=conv_w[l], conv_b=conv_b[l], lru_wa=lru_wa[l],
                 lru_ba=lru_ba[l], lru_wx=lru_wx[l], lru_bx=lru_bx[l], lru_lam=lru_lam[l],
                 s5_lam_re=s5_lam_re[l], s5_lam_im=s5_lam_im[l], s5_log_dt=s5_log_dt[l],
                 s5_b_re=s5_b_re[l], s5_b_im=s5_b_im[l], s5_c_re=s5_c_re[l], s5_c_im=s5_c_im[l],
                 s5_d=s5_d[l], s5_w_glu=s5_w_glu[l], s5_b_glu=s5_b_glu[l], w_branch=w_branch[l],
                 w_out=w_out[l])
        x, ctx = hybrid_layer(x, ctx, c, c_ctx, p, l, l < DEPTH - 1)
    return rmsnorm(x, final_g)
```

```python
import functools
import math

import jax
import jax.numpy as jnp
from jax import lax
from jax.experimental import pallas as pl
from jax.experimental.pallas import tpu as pltpu

F32 = jnp.float32
BF16 = jnp.bfloat16

EPS = 1e-6
GRID_W = 64
ATT_HEADS = 4
ATT_QK_DIM = 64
ATT_V_DIM = 2 * ATT_QK_DIM
BRANCH_W = 512
N_BRANCH = 3
ROPE_BASE = 10000.0
LRU_BLOCKS = 8
LRU_C = 8.0
CONV_W = 4
S5_GROUP = 16
S5_GROUPS = BRANCH_W // S5_GROUP
S5_STATE = 64
S5_LANES = S5_GROUPS * S5_STATE
N_SEG = 8 + 2 * N_BRANCH

SUBLANES = 8
LANES = 128
CHUNK_T = 64
VMEM_LIMIT = 56 * 1024 * 1024

LOG2E = math.log2(math.e)


def _cparams(sem):
    return pltpu.CompilerParams(dimension_semantics=sem, vmem_limit_bytes=VMEM_LIMIT)


def _sigmoid(z):
    return 1.0 / (1.0 + jnp.exp(-z))


def _silu(z):
    return z * _sigmoid(z)


def _mod_kernel(c_ref, w_ref, b_ref, o_ref):
    c = c_ref[...]
    o_ref[...] = jnp.dot(_silu(c), w_ref[...], preferred_element_type=F32,
                         precision=lax.Precision.HIGHEST) + b_ref[...]


def _ada_mod(cond, w_mod, b_mod):
    depth, d, d3 = w_mod.shape
    return pl.pallas_call(
        _mod_kernel,
        grid=(depth, d3 // d),
        in_specs=[pl.BlockSpec((2 * SUBLANES, d), lambda l, j: (0, 0)),
                  pl.BlockSpec((None, d, d), lambda l, j: (l, 0, j)),
                  pl.BlockSpec((None, 1, d), lambda l, j: (l, 0, j))],
        out_specs=pl.BlockSpec((None, 2 * SUBLANES, d), lambda l, j: (l, 0, j)),
        out_shape=jax.ShapeDtypeStruct((depth, 2 * SUBLANES, d3), F32),
        compiler_params=_cparams(("parallel", "parallel")),
        name="ada_mod",
    )(cond, w_mod, b_mod.reshape(depth, 1, d3))


def _inproj_kernel(x_ref, g_ref, mod_ref, cos_ref, sin_ref, w_ref,
                   q_ref, k_ref, v_ref, za_ref, xr_ref, zr_ref, us_ref, zs_ref, gl_ref,
                   h_scr, *, q_scale):
    x = x_ref[...]
    tm, d = x.shape
    y = x * lax.rsqrt(jnp.mean(x * x, axis=-1, keepdims=True) + EPS) * g_ref[...]
    y = y.reshape(tm // SUBLANES, SUBLANES, d)
    h = y * (1.0 + mod_ref[1])[None] + mod_ref[0][None]
    h_scr[...] = h.reshape(tm, d).astype(BF16)

    def proj(j):
        return jnp.dot(h_scr[...], w_ref[:, j * BRANCH_W:(j + 1) * BRANCH_W],
                       preferred_element_type=F32)

    cos = cos_ref[...]
    sin = sin_ref[...]

    def rope_store(a, ref, scale):
        for hh in range(ATT_HEADS):
            ah = a[:, hh * LANES:(hh + 1) * LANES]
            r = ah * cos + pltpu.roll(ah, LANES // 2, axis=1) * sin
            ref[:, hh * LANES:(hh + 1) * LANES] = (r * scale).astype(BF16)

    rope_store(proj(0), q_ref, q_scale)
    rope_store(proj(1), k_ref, 1.0)
    for j, ref in ((2, v_ref), (3, za_ref), (4, xr_ref), (5, zr_ref), (6, us_ref), (7, zs_ref)):
        ref[...] = proj(j).astype(BF16)
    for j in range(2 * N_BRANCH):
        gl_ref[:, j * BRANCH_W:(j + 1) * BRANCH_W] = proj(8 + j).astype(BF16)


def _in_proj(x, norm_g, mod, cos, sin, w_in, n_ctx_chunks):
    rows, d = x.shape
    tm = CHUNK_T * SUBLANES
    nch = rows // tm
    row_spec = lambda w: pl.BlockSpec((tm, w), lambda i: (i, 0))
    out_w = [BRANCH_W] * 8 + [N_BRANCH * d]
    return pl.pallas_call(
        functools.partial(_inproj_kernel, q_scale=ATT_QK_DIM ** -0.5 * LOG2E),
        grid=(nch,),
        in_specs=[row_spec(d),
                  pl.BlockSpec((1, d), lambda i: (0, 0)),
                  pl.BlockSpec((None, 3, SUBLANES, d),
                               lambda i: (jnp.where(i < n_ctx_chunks, 0, 1), 0, 0, 0)),
                  row_spec(LANES), row_spec(LANES),
                  pl.BlockSpec(w_in.shape, lambda i: (0, 0), pipeline_mode=pl.Buffered(1))],
        out_specs=[row_spec(w) for w in out_w],
        out_shape=[jax.ShapeDtypeStruct((rows, w), BF16) for w in out_w],
        scratch_shapes=[pltpu.VMEM((tm, d), BF16)],
        compiler_params=_cparams(("parallel",)),
        name="in_proj",
    )(x, norm_g, mod, cos, sin, w_in)


def _attn_kernel(lamqk_ref, g_ref, q_ref, k_ref, v_ref, o_ref,
                 s_scr, m_scr, acc_scr, vext_scr, *, tq, n_ctx_tiles, n_ctx, n_all, lam_init):
    i = pl.program_id(2)

    @pl.when(i == 0)
    def _():
        vext_scr[:, :ATT_V_DIM] = v_ref[...]
        vext_scr[:, ATT_V_DIM:] = jnp.ones((n_all, ATT_V_DIM), BF16)

    lq = lamqk_ref[...]
    lam = (jnp.exp(jnp.sum(lq[0:1] * lq[1:2], axis=-1, keepdims=True))
           - jnp.exp(jnp.sum(lq[2:3] * lq[3:4], axis=-1, keepdims=True)) + lam_init)

    def run(nk):
        kc = math.gcd(nk, 512)
        nchunks = nk // kc
        q = q_ref[...]
        lane = lax.broadcasted_iota(jnp.int32, q.shape, 1)
        comp0 = (lane % (LANES // 2)) < (LANES // 4)
        zero = jnp.zeros_like(q)
        qq = jnp.concatenate([jnp.where(comp0, q, zero), jnp.where(comp0, zero, q)], axis=0)
        m_scr[...] = jnp.full(m_scr.shape, -jnp.inf, F32)

        def scores(c, carry):
            off = pl.multiple_of(c * kc, kc)
            s = lax.dot_general(qq, k_ref[pl.ds(off, kc), :], (((1,), (1,)), ((), ())),
                                preferred_element_type=F32)
            s_scr[c, :, :kc] = s
            mm = m_scr[...]
            for jj in range(kc // LANES):
                mm = jnp.maximum(mm, s[:, jj * LANES:(jj + 1) * LANES])
            m_scr[...] = mm
            return carry

        lax.fori_loop(0, nchunks, scores, 0)
        m = jnp.max(m_scr[...], axis=-1, keepdims=True)
        acc_scr[...] = jnp.zeros(acc_scr.shape, F32)

        def weighted(c, carry):
            off = pl.multiple_of(c * kc, kc)
            p = jnp.exp2(s_scr[c, :, :kc] - m).astype(BF16)
            acc_scr[...] += jnp.dot(p, vext_scr[pl.ds(off, kc), :], preferred_element_type=F32)
            return carry

        lax.fori_loop(0, nchunks, weighted, 0)
        acc = acc_scr[...]
        o = acc[:, :ATT_V_DIM] / acc[:, ATT_V_DIM:]
        o = o[:tq] - lam * o[tq:]
        o = o * lax.rsqrt(jnp.mean(o * o, axis=-1, keepdims=True) + EPS) * g_ref[...]
        o_ref[...] = (o * (1.0 - lam_init)).astype(BF16)

    @pl.when(i < n_ctx_tiles)
    def _():
        run(n_ctx)

    @pl.when(i >= n_ctx_tiles)
    def _():
        run(n_all)


def _attention(q, k, v, lam_qk, subln_g, lam_init, batch, n_ctx):
    rows, width = q.shape
    s_all = rows // batch
    tq = min(256, n_ctx)
    kc = math.gcd(s_all, 512)
    view = lambda a: a.reshape(s_all, batch * width)
    col = lambda b, h, i: b * ATT_HEADS + h
    out = pl.pallas_call(
        functools.partial(_attn_kernel, tq=tq, n_ctx_tiles=n_ctx // tq, n_ctx=n_ctx, n_all=s_all,
                          lam_init=lam_init),
        grid=(batch, ATT_HEADS, s_all // tq),
        in_specs=[pl.BlockSpec((4, ATT_QK_DIM), lambda b, h, i: (0, 0)),
                  pl.BlockSpec((1, ATT_V_DIM), lambda b, h, i: (0, 0)),
                  pl.BlockSpec((tq, LANES), lambda b, h, i: (i, col(b, h, i))),
                  pl.BlockSpec((s_all, LANES), lambda b, h, i: (0, col(b, h, i))),
                  pl.BlockSpec((s_all, LANES), lambda b, h, i: (0, col(b, h, i)))],
        out_specs=pl.BlockSpec((tq, LANES), lambda b, h, i: (i, col(b, h, i))),
        out_shape=jax.ShapeDtypeStruct((s_all, batch * width), BF16),
        scratch_shapes=[pltpu.VMEM((s_all // kc, 2 * tq, kc), F32),
                        pltpu.VMEM((2 * tq, LANES), F32),
                        pltpu.VMEM((2 * tq, 2 * ATT_V_DIM), F32),
                        pltpu.VMEM((s_all, 2 * ATT_V_DIM), BF16)],
        compiler_params=_cparams(("parallel", "parallel", "arbitrary")),
        name="diff_attention",
    )(lam_qk, subln_g, view(q), view(k), view(v))
    return out.reshape(rows, width)


def _scan_chunk(d, j, n_ctx_chunks, nch):
    back = jnp.where(j < n_ctx_chunks, n_ctx_chunks - 1 - j, nch - 1 + n_ctx_chunks - j)
    return jnp.where(d == 0, j, back)


def _lru_kernel(xp_ref, xc_ref, xn_ref, cw_ref, cb_ref, wg_ref, bg_ref, lam_ref, o_ref,
                a_scr, b_scr, h_scr, *, n_ctx_chunks, nch):
    d = pl.program_id(0)
    j = pl.program_id(1)
    c = _scan_chunk(d, j, n_ctx_chunks, nch)
    tm = xc_ref.shape[0]
    first = jnp.logical_or(c == 0, c == n_ctx_chunks)
    last = jnp.logical_or(c == n_ctx_chunks - 1, c == nch - 1)
    xp = xp_ref[...].astype(F32) * jnp.where(first, 0.0, 1.0)
    xn = xn_ref[...].astype(F32)[:SUBLANES] * jnp.where(last, 0.0, 1.0)
    xe = jnp.concatenate([xp, xc_ref[...].astype(F32), xn], axis=0)
    u = cb_ref[...]
    for tap in range(CONV_W):
        u = u + cw_ref[tap:tap + 1, :] * xe[tap * SUBLANES:tap * SUBLANES + tm]
    gates = jnp.dot(u.astype(BF16), wg_ref[...], preferred_element_type=F32) + bg_ref[...]
    r = _sigmoid(gates[:, :BRANCH_W])
    gi = _sigmoid(gates[:, BRANCH_W:])
    z = -lam_ref[...]
    softplus = jnp.maximum(z, 0.0) + jnp.log(1.0 + jnp.exp(-jnp.abs(z)))
    a = jnp.exp((-LRU_C * softplus) * r)
    a_scr[...] = a
    b_scr[...] = jnp.sqrt(1.0 - a * a) * (gi * u)

    @pl.when(j == 0)
    def _():
        h_scr[...] = jnp.zeros(h_scr.shape, F32)

    nt = tm // SUBLANES

    def step(t, h):
        te = jnp.where(d == 0, t, nt - 1 - t)
        r0 = pl.multiple_of(te * SUBLANES, SUBLANES)
        h = a_scr[pl.ds(r0, SUBLANES), :] * h + b_scr[pl.ds(r0, SUBLANES), :]
        o_ref[pl.ds(r0, SUBLANES), :] = h
        return h

    h_scr[...] = lax.fori_loop(0, nt, step, h_scr[...], unroll=8)


def _rglru(xr, conv_w, conv_b, w_gate, b_gate, lru_lam, n_ctx_chunks):
    rows, width = xr.shape
    tm = CHUNK_T * SUBLANES
    nch = rows // tm
    halo = 2 * SUBLANES
    per = tm // halo
    chunk = lambda d, j: _scan_chunk(d, j, n_ctx_chunks, nch)
    return pl.pallas_call(
        functools.partial(_lru_kernel, n_ctx_chunks=n_ctx_chunks, nch=nch),
        grid=(2, nch),
        in_specs=[pl.BlockSpec((halo, width), lambda d, j: (jnp.maximum(chunk(d, j) * per - 1, 0), 0)),
                  pl.BlockSpec((tm, width), lambda d, j: (chunk(d, j), 0)),
                  pl.BlockSpec((halo, width),
                               lambda d, j: (jnp.minimum((chunk(d, j) + 1) * per, nch * per - 1), 0)),
                  pl.BlockSpec((CONV_W, width), lambda d, j: (0, 0)),
                  pl.BlockSpec((1, width), lambda d, j: (0, 0)),
                  pl.BlockSpec((None, width, 2 * width), lambda d, j: (d, 0, 0)),
                  pl.BlockSpec((None, 1, 2 * width), lambda d, j: (d, 0, 0)),
                  pl.BlockSpec((None, 1, width), lambda d, j: (d, 0, 0))],
        out_specs=pl.BlockSpec((None, tm, width), lambda d, j: (d, chunk(d, j), 0)),
        out_shape=jax.ShapeDtypeStruct((2, rows, width), F32),
        scratch_shapes=[pltpu.VMEM((tm, width), F32), pltpu.VMEM((tm, width), F32),
                        pltpu.VMEM((SUBLANES, width), F32)],
        compiler_params=_cparams(("arbitrary", "arbitrary")),
        name="rglru",
    )(xr, xr, xr, conv_w, conv_b, w_gate, b_gate, lru_lam)


def _s5_kernel(us_ref, bd_ref, cd_ref, ar_ref, ai_ref, o_ref, drv_scr, h_scr):
    d = pl.program_id(0)
    j = pl.program_id(1)
    tm = us_ref.shape[0]
    nt = tm // SUBLANES
    blk = 512
    us = us_ref[...]
    for nb in range(2 * S5_LANES // blk):
        drv_scr[:, nb * blk:(nb + 1) * blk] = jnp.dot(us, bd_ref[:, nb * blk:(nb + 1) * blk],
                                                      preferred_element_type=F32)

    @pl.when(j == 0)
    def _():
        h_scr[...] = jnp.zeros(h_scr.shape, F32)

    for cb in range(S5_LANES // blk):
        re = slice(cb * blk, (cb + 1) * blk)
        im = slice(S5_LANES + cb * blk, S5_LANES + (cb + 1) * blk)
        ar = jnp.broadcast_to(ar_ref[:, re], (SUBLANES, blk))
        ai = jnp.broadcast_to(ai_ref[:, re], (SUBLANES, blk))

        def step(t, h, re=re, im=im, ar=ar, ai=ai):
            hr, hi = h
            te = jnp.where(d == 0, t, nt - 1 - t)
            rws = pl.ds(pl.multiple_of(te * SUBLANES, SUBLANES), SUBLANES)
            nr = ar * hr - ai * hi + drv_scr[rws, re]
            ni = ar * hi + ai * hr + drv_scr[rws, im]
            drv_scr[rws, re] = nr
            drv_scr[rws, im] = ni
            return nr, ni

        hr, hi = lax.fori_loop(0, nt, step, (h_scr[:, re], h_scr[:, im]), unroll=4)
        h_scr[:, re] = hr
        h_scr[:, im] = hi

    acc = jnp.zeros(o_ref.shape, F32)
    for kk in range(2 * S5_LANES // blk):
        acc = acc + jnp.dot(drv_scr[:, kk * blk:(kk + 1) * blk].astype(BF16),
                            cd_ref[kk * blk:(kk + 1) * blk, :], preferred_element_type=F32)
    o_ref[...] = acc


def _s5(us, bd, cd, ar, ai, n_ctx_chunks):
    rows, width = us.shape
    tm = CHUNK_T * SUBLANES
    nch = rows // tm
    chunk = lambda d, j: _scan_chunk(d, j, n_ctx_chunks, nch)
    return pl.pallas_call(
        _s5_kernel,
        grid=(2, nch),
        in_specs=[pl.BlockSpec((tm, width), lambda d, j: (chunk(d, j), 0)),
                  pl.BlockSpec((None, width, 2 * S5_LANES), lambda d, j: (d, 0, 0)),
                  pl.BlockSpec((2 * S5_LANES, width), lambda d, j: (0, 0)),
                  pl.BlockSpec((None, 1, S5_LANES), lambda d, j: (d, 0, 0)),
                  pl.BlockSpec((None, 1, S5_LANES), lambda d, j: (d, 0, 0))],
        out_specs=pl.BlockSpec((None, tm, width), lambda d, j: (d, chunk(d, j), 0)),
        out_shape=jax.ShapeDtypeStruct((2, rows, width), F32),
        scratch_shapes=[pltpu.VMEM((tm, 2 * S5_LANES), F32),
                        pltpu.VMEM((SUBLANES, 2 * S5_LANES), F32)],
        compiler_params=_cparams(("arbitrary", "arbitrary")),
        name="s5_scan",
    )(us, bd, cd, ar, ai)


def _s5_disc_kernel(lr_ref, li_ref, ldt_ref, ar_ref, ai_ref, cr_ref, ci_ref):
    lr = lr_ref[...]
    li = li_ref[...]
    dt = jnp.exp(ldt_ref[...])
    mag = jnp.exp(lr * dt)
    ar = mag * jnp.cos(li * dt)
    ai = mag * jnp.sin(li * dt)
    den = lr * lr + li * li
    nr = ar - 1.0
    ar_ref[...] = ar
    ai_ref[...] = ai
    cr_ref[...] = (nr * lr + ai * li) / den
    ci_ref[...] = (ai * lr - nr * li) / den


def _s5_discretise(lam_re, lam_im, log_dt):
    n = lam_re.shape[0]
    shp = jax.ShapeDtypeStruct((n, S5_STATE), F32)
    return pl.pallas_call(_s5_disc_kernel, out_shape=[shp] * 4, name="s5_discretise")(
        lam_re, lam_im, jnp.broadcast_to(log_dt, (n, S5_STATE)))


def _gelu_tanh(y):
    return 0.5 * y * (1.0 + jnp.tanh(math.sqrt(2.0 / math.pi) * (y + 0.044715 * (y * y * y))))


def _merge_kernel(x_ref, mod_ref, ya_ref, hl_ref, ys_ref, us_ref, za_ref, zr_ref, zs_ref, gl_ref,
                  sd_ref, wglu_ref, bglu_ref, wbr_ref, wout_ref, o_ref):
    tm, d = x_ref.shape
    y = ys_ref[0] + ys_ref[1] + sd_ref[...] * us_ref[...].astype(F32)
    g = _gelu_tanh(y)
    ys = g * _sigmoid(jnp.dot(g.astype(BF16), wglu_ref[...], preferred_element_type=F32)
                      + bglu_ref[...])
    yr = hl_ref[0] + hl_ref[1]
    ya = ya_ref[...].astype(F32)
    acc = jnp.zeros((tm, d), F32)
    for n, (yn, zn_ref) in enumerate(((ya, za_ref), (yr, zr_ref), (ys, zs_ref))):
        yb = jnp.dot((yn * _silu(zn_ref[...].astype(F32))).astype(BF16), wbr_ref[n],
                     preferred_element_type=F32)
        acc = acc + _sigmoid(gl_ref[:, n * d:(n + 1) * d].astype(F32)) * yb
    upd = jnp.dot(acc.astype(BF16), wout_ref[...], preferred_element_type=F32)
    split = lambda a: a.reshape(tm // SUBLANES, SUBLANES, d)
    o_ref[...] = (split(x_ref[...]) + mod_ref[2][None] * split(upd)).reshape(tm, d)


def _merge(x, mod, ya, hl, ys, us, za, zr, zs, gl, s5_d, w_glu, b_glu, w_branch, w_out, n_ctx_chunks):
    rows, d = x.shape
    tm = CHUNK_T * SUBLANES
    nch = rows // tm
    w = BRANCH_W
    row_spec = lambda n: pl.BlockSpec((tm, n), lambda i: (i, 0))
    pair_spec = pl.BlockSpec((2, tm, w), lambda i: (0, i, 0))
    full = lambda a: pl.BlockSpec(a.shape, lambda i: (0,) * a.ndim)
    return pl.pallas_call(
        _merge_kernel,
        grid=(nch,),
        in_specs=[row_spec(d),
                  pl.BlockSpec((None, 3, SUBLANES, d),
                               lambda i: (jnp.where(i < n_ctx_chunks, 0, 1), 0, 0, 0)),
                  row_spec(w), pair_spec, pair_spec, row_spec(w), row_spec(w), row_spec(w),
                  row_spec(w), row_spec(N_BRANCH * d),
                  full(s5_d), full(w_glu), full(b_glu), full(w_branch), full(w_out)],
        out_specs=row_spec(d),
        out_shape=jax.ShapeDtypeStruct((rows, d), F32),
        compiler_params=_cparams(("parallel",)),
        name="gated_merge",
    )(x, mod, ya, hl, ys, us, za, zr, zs, gl, s5_d, w_glu, b_glu, w_branch, w_out)


def _final_kernel(x_ref, g_ref, o_ref):
    x = x_ref[...]
    o_ref[...] = x * lax.rsqrt(jnp.mean(x * x, axis=-1, keepdims=True) + EPS) * g_ref[...]


def _final_norm(x, g, n_ctx_chunks):
    rows, d = x.shape
    tm = CHUNK_T * SUBLANES
    nlat = rows // tm - n_ctx_chunks
    return pl.pallas_call(
        _final_kernel,
        grid=(nlat,),
        in_specs=[pl.BlockSpec((tm, d), lambda i: (i + n_ctx_chunks, 0)),
                  pl.BlockSpec((1, d), lambda i: (0, 0))],
        out_specs=pl.BlockSpec((tm, d), lambda i: (i, 0)),
        out_shape=jax.ShapeDtypeStruct((nlat * tm, d), F32),
        compiler_params=_cparams(("parallel",)),
        name="final_norm",
    )(x, g)


def _rope_tables(n_lat, n_ctx, batch):
    n_freq = ATT_QK_DIM // 4
    tl = jnp.arange(n_lat)
    inv = ROPE_BASE ** (-jnp.arange(n_freq, dtype=F32) / n_freq)
    ang = jnp.concatenate([(tl // GRID_W).astype(F32)[:, None] * inv,
                           (tl % GRID_W).astype(F32)[:, None] * inv], axis=-1)
    cos = jnp.tile(jnp.cos(ang), (1, 4))
    sin = jnp.tile(jnp.sin(ang), (1, 4)) * jnp.where(jnp.arange(LANES) < LANES // 2, -1.0, 1.0)
    cos = jnp.concatenate([jnp.ones((n_ctx, LANES), F32), cos], axis=0)
    sin = jnp.concatenate([jnp.zeros((n_ctx, LANES), F32), sin], axis=0)
    rep = lambda a: jnp.broadcast_to(a[:, None, :], (a.shape[0], batch, LANES)).reshape(-1, LANES)
    return rep(cos), rep(sin)


def _rope_col_perm():
    j = jnp.arange(LANES)
    within = ((j // 32) % 2) * ATT_QK_DIM + 2 * (j % 32) + j // 64
    return (jnp.arange(ATT_HEADS)[:, None] * LANES + within[None, :]).reshape(-1)


def _block_diag(w):
    n, c, _ = w.shape
    return jnp.einsum('ncd,nm->ncmd', w, jnp.eye(n, dtype=w.dtype)).reshape(n * c, n * c)


def kernel(x, c, ctx, c_ctx, w_mod, b_mod, norm_g, w_in, lam_qk, subln_g, conv_w, conv_b, lru_wa, lru_ba,
           lru_wx, lru_bx, lru_lam, s5_lam_re, s5_lam_im, s5_log_dt, s5_b_re, s5_b_im, s5_c_re, s5_c_im,
           s5_d, s5_w_glu, s5_b_glu, w_branch, w_out, final_g):
    batch, n_lat, d = x.shape
    n_ctx = ctx.shape[1]
    depth = w_mod.shape[0]
    assert batch == SUBLANES and n_lat % CHUNK_T == 0 and n_ctx % CHUNK_T == 0
    n_ctx_chunks = n_ctx // CHUNK_T
    s_all = n_ctx + n_lat
    rows = s_all * batch

    xs = jnp.concatenate([jnp.swapaxes(ctx, 0, 1), jnp.swapaxes(x, 0, 1)], axis=0).reshape(rows, d)

    cond = jnp.zeros((2 * SUBLANES, d), F32).at[:batch].set(c).at[batch].set(c_ctx)
    m = _ada_mod(cond, w_mod, b_mod).reshape(depth, 2 * SUBLANES, 3, d)
    mod = jnp.stack([jnp.broadcast_to(m[:, batch][:, None], (depth, batch, 3, d)), m[:, :batch]], axis=1)
    mod = jnp.swapaxes(mod, 2, 3)

    cos, sin = _rope_tables(n_lat, n_ctx, batch)
    perm = _rope_col_perm()
    qk_w = ATT_HEADS * 2 * ATT_QK_DIM
    w_in_p = jnp.concatenate([w_in[:, :, :qk_w][:, :, perm], w_in[:, :, qk_w:2 * qk_w][:, :, perm],
                              w_in[:, :, 2 * qk_w:]], axis=-1).astype(BF16)

    n_dir = depth * 2 * S5_GROUPS
    ar, ai, cr, ci = _s5_discretise(s5_lam_re.reshape(n_dir, S5_STATE), s5_lam_im.reshape(n_dir, S5_STATE),
                                    s5_log_dt.reshape(n_dir, 1))
    gshape = (depth, 2, S5_GROUPS, S5_STATE)
    ar, ai, cr, ci = (a.reshape(gshape) for a in (ar, ai, cr, ci))
    bbr = cr[..., None] * s5_b_re - ci[..., None] * s5_b_im
    bbi = cr[..., None] * s5_b_im + ci[..., None] * s5_b_re
    eye_g = jnp.eye(S5_GROUPS, dtype=F32)
    drive = lambda bb: jnp.einsum('ldgph,gk->ldghkp', bb, eye_g).reshape(depth, 2, BRANCH_W, S5_LANES)
    bd = jnp.concatenate([drive(bbr), drive(bbi)], axis=-1).astype(BF16)
    read = lambda cc: jnp.einsum('lghp,gk->lgpkh', cc, eye_g).reshape(depth, S5_LANES, BRANCH_W)
    cd = jnp.concatenate([read(s5_c_re), -read(s5_c_im)], axis=1).astype(BF16)
    ar = ar.reshape(depth, 2, 1, S5_LANES)
    ai = ai.reshape(depth, 2, 1, S5_LANES)

    bdiag = jax.vmap(jax.vmap(_block_diag))
    w_gate = jnp.concatenate([bdiag(lru_wa), bdiag(lru_wx)], axis=-1).astype(BF16)
    b_gate = jnp.concatenate([lru_ba, lru_bx], axis=-1)[:, :, None, :]

    for l in range(depth):
        lam_init = 0.8 - 0.6 * math.exp(-0.3 * l)
        q, k, v, za, xr, zr, us, zs, gl = _in_proj(xs, norm_g[l][None], mod[l], cos, sin, w_in_p[l],
                                                   n_ctx_chunks)
        ya = _attention(q, k, v, lam_qk[l], subln_g[l][None], lam_init, batch, n_ctx)
        hl = _rglru(xr, conv_w[l], conv_b[l][None], w_gate[l], b_gate[l], lru_lam[l][:, None, :],
                    n_ctx_chunks)
        ys = _s5(us, bd[l], cd[l], ar[l], ai[l], n_ctx_chunks)
        xs = _merge(xs, mod[l], ya, hl, ys, us, za, zr, zs, gl, s5_d[l][None], s5_w_glu[l].astype(BF16),
                    s5_b_glu[l][None], w_branch[l].astype(BF16), w_out[l].astype(BF16), n_ctx_chunks)

    out = _final_norm(xs, final_g[None], n_ctx_chunks)
    return jnp.swapaxes(out.reshape(n_lat, batch, d), 0, 1)
```

```python
import functools
import math

import jax
import jax.numpy as jnp
from jax import lax
from jax.experimental import pallas as pl
from jax.experimental.pallas import tpu as pltpu

F32 = jnp.float32
BF16 = jnp.bfloat16

EPS = 1e-6
GRID_W = 64
ATT_HEADS = 4
ATT_QK_DIM = 64
ATT_V_DIM = 2 * ATT_QK_DIM
BRANCH_W = 512
N_BRANCH = 3
ROPE_BASE = 10000.0
LRU_BLOCKS = 8
LRU_C = 8.0
CONV_W = 4
S5_GROUP = 16
S5_GROUPS = BRANCH_W // S5_GROUP
S5_STATE = 64
S5_LANES = S5_GROUPS * S5_STATE
S5_BLOCK_GROUPS = 8
S5_BLOCKS = S5_GROUPS // S5_BLOCK_GROUPS
S5_HALF = S5_BLOCK_GROUPS * S5_STATE

SUBLANES = 8
LANES = 128
CHUNK_T = 64
KEY_CHUNK = 256
VMEM_LIMIT = 56 * 1024 * 1024

LOG2E = math.log2(math.e)


def _cparams(sem):
    return pltpu.CompilerParams(dimension_semantics=sem, vmem_limit_bytes=VMEM_LIMIT)


def _sigmoid(z):
    return 1.0 / (1.0 + jnp.exp(-z))


def _silu(z):
    return z * _sigmoid(z)


def _mod_kernel(c_ref, w_ref, b_ref, o_ref):
    c = c_ref[...]
    o_ref[...] = jnp.dot(_silu(c), w_ref[...], preferred_element_type=F32,
                         precision=lax.Precision.HIGHEST) + b_ref[...]


def _ada_mod(cond, w_mod, b_mod):
    depth, d, d3 = w_mod.shape
    return pl.pallas_call(
        _mod_kernel,
        grid=(depth, d3 // d),
        in_specs=[pl.BlockSpec((2 * SUBLANES, d), lambda l, j: (0, 0)),
                  pl.BlockSpec((None, d, d), lambda l, j: (l, 0, j)),
                  pl.BlockSpec((None, 1, d), lambda l, j: (l, 0, j))],
        out_specs=pl.BlockSpec((None, 2 * SUBLANES, d), lambda l, j: (l, 0, j)),
        out_shape=jax.ShapeDtypeStruct((depth, 2 * SUBLANES, d3), F32),
        compiler_params=_cparams(("parallel", "parallel")),
        name="ada_mod",
    )(cond, w_mod, b_mod.reshape(depth, 1, d3))


def _inproj_kernel(x_ref, g_ref, mod_ref, cos_ref, sin_ref, w_ref,
                   q_ref, k_ref, v_ref, za_ref, xr_ref, zr_ref, us_ref, zs_ref, gl_ref,
                   h_scr, *, q_scale):
    x = x_ref[...]
    tm, d = x.shape
    y = x * lax.rsqrt(jnp.mean(x * x, axis=-1, keepdims=True) + EPS) * g_ref[...]
    y = y.reshape(tm // SUBLANES, SUBLANES, d)
    h = y * (1.0 + mod_ref[1])[None] + mod_ref[0][None]
    h_scr[...] = h.reshape(tm, d).astype(BF16)

    def proj(j):
        return jnp.dot(h_scr[...], w_ref[:, j * BRANCH_W:(j + 1) * BRANCH_W],
                       preferred_element_type=F32)

    cos = cos_ref[...]
    sin = sin_ref[...]

    def rope_store(a, ref, scale):
        for hh in range(ATT_HEADS):
            ah = a[:, hh * LANES:(hh + 1) * LANES]
            r = ah * cos + pltpu.roll(ah, LANES // 2, axis=1) * sin
            ref[:, hh * LANES:(hh + 1) * LANES] = (r * scale).astype(BF16)

    rope_store(proj(0), q_ref, q_scale)
    rope_store(proj(1), k_ref, 1.0)
    for j, ref in ((2, v_ref), (3, za_ref), (4, xr_ref), (5, zr_ref), (6, us_ref), (7, zs_ref)):
        ref[...] = proj(j).astype(BF16)
    for j in range(2 * N_BRANCH):
        gl_ref[:, j * BRANCH_W:(j + 1) * BRANCH_W] = proj(8 + j).astype(BF16)


def _in_proj(x, norm_g, mod, cos, sin, w_in, n_ctx_chunks):
    rows, d = x.shape
    tm = CHUNK_T * SUBLANES
    nch = rows // tm
    row_spec = lambda w: pl.BlockSpec((tm, w), lambda i: (i, 0))
    out_w = [BRANCH_W] * 8 + [N_BRANCH * d]
    return pl.pallas_call(
        functools.partial(_inproj_kernel, q_scale=ATT_QK_DIM ** -0.5 * LOG2E),
        grid=(nch,),
        in_specs=[row_spec(d),
                  pl.BlockSpec((1, d), lambda i: (0, 0)),
                  pl.BlockSpec((None, 3, SUBLANES, d),
                               lambda i: (jnp.where(i < n_ctx_chunks, 0, 1), 0, 0, 0)),
                  row_spec(LANES), row_spec(LANES),
                  pl.BlockSpec(w_in.shape, lambda i: (0, 0), pipeline_mode=pl.Buffered(1))],
        out_specs=[row_spec(w) for w in out_w],
        out_shape=[jax.ShapeDtypeStruct((rows, w), BF16) for w in out_w],
        scratch_shapes=[pltpu.VMEM((tm, d), BF16)],
        compiler_params=_cparams(("parallel",)),
        name="in_proj",
    )(x, norm_g, mod, cos, sin, w_in)


def _attn_kernel(lamqk_ref, g_ref, qc_ref, qn_ref, k_ref, v_ref, o_ref,
                 s0_scr, s1_scr, m0_scr, m1_scr, vext_scr, *, tq, n_ctx_tiles, n_ctx, n_all, lam_init):
    i = pl.program_id(2)

    @pl.when(i == 0)
    def _():
        vext_scr[:, :ATT_V_DIM] = v_ref[...]
        vext_scr[:, ATT_V_DIM:] = jnp.ones((n_all, ATT_V_DIM), BF16)

    lq = lamqk_ref[...]
    lam = (jnp.exp(jnp.sum(lq[0:1] * lq[1:2], axis=-1, keepdims=True))
           - jnp.exp(jnp.sum(lq[2:3] * lq[3:4], axis=-1, keepdims=True)) + lam_init)

    def chunks(nk):
        return [(off, min(KEY_CHUNK, nk - off)) for off in range(0, nk, KEY_CHUNK)]

    def scores(q_ref, nk, s_dst, m_dst):
        q = q_ref[...]
        lane = lax.broadcasted_iota(jnp.int32, q.shape, 1)
        comp0 = (lane % (LANES // 2)) < (LANES // 4)
        zero = jnp.zeros_like(q)
        for c, qc in enumerate((jnp.where(comp0, q, zero), jnp.where(comp0, zero, q))):
            rows = slice(c * tq, (c + 1) * tq)
            mm = None
            for off, kc in chunks(nk):
                s = lax.dot_general(qc, k_ref[off:off + kc, :], (((1,), (1,)), ((), ())),
                                    preferred_element_type=F32)
                s_dst[rows, off:off + kc] = s
                for jj in range(kc // LANES):
                    sj = s[:, jj * LANES:(jj + 1) * LANES]
                    mm = sj if mm is None else jnp.maximum(mm, sj)
            m_dst[rows, :] = jnp.broadcast_to(jnp.max(mm, axis=-1, keepdims=True), (tq, LANES))

    def weighted(nk, s_src, m_src):
        os = []
        for c in range(2):
            rows = slice(c * tq, (c + 1) * tq)
            m = m_src[rows, :]
            acc = jnp.zeros((tq, 2 * ATT_V_DIM), F32)
            for off, kc in chunks(nk):
                mrep = jnp.concatenate([m] * (kc // LANES), axis=1)
                p = jnp.exp2(s_src[rows, off:off + kc] - mrep).astype(BF16)
                acc = acc + jnp.dot(p, vext_scr[off:off + kc, :], preferred_element_type=F32)
            os.append(acc[:, :ATT_V_DIM] / acc[:, ATT_V_DIM:])
        o = os[0] - lam * os[1]
        o = o * lax.rsqrt(jnp.mean(o * o, axis=-1, keepdims=True) + EPS) * g_ref[...]
        o_ref[...] = (o * (1.0 - lam_init)).astype(BF16)

    slots = ((s0_scr, m0_scr), (s1_scr, m1_scr))
    for par in range(2):
        cur, nxt = slots[par], slots[1 - par]
        mine = i % 2 == par

        @pl.when(jnp.logical_and(mine, i < n_ctx_tiles))
        def _(cur=cur):
            scores(qc_ref, n_ctx, *cur)
            weighted(n_ctx, *cur)

        @pl.when(jnp.logical_and(mine, i == n_ctx_tiles - 1))
        def _(nxt=nxt):
            scores(qn_ref, n_all, *nxt)

        @pl.when(jnp.logical_and(mine, i >= n_ctx_tiles))
        def _(cur=cur, nxt=nxt):
            weighted(n_all, *cur)
            scores(qn_ref, n_all, *nxt)


def _attention(q, k, v, lam_qk, subln_g, lam_init, batch, n_ctx):
    rows, width = q.shape
    s_all = rows // batch
    tq = min(256, n_ctx)
    nt = s_all // tq
    view = lambda a: a.reshape(s_all, batch * width)
    col = lambda b, h, i: b * ATT_HEADS + h
    out = pl.pallas_call(
        functools.partial(_attn_kernel, tq=tq, n_ctx_tiles=n_ctx // tq, n_ctx=n_ctx, n_all=s_all,
                          lam_init=lam_init),
        grid=(batch, ATT_HEADS, nt),
        in_specs=[pl.BlockSpec((4, ATT_QK_DIM), lambda b, h, i: (0, 0)),
                  pl.BlockSpec((1, ATT_V_DIM), lambda b, h, i: (0, 0)),
                  pl.BlockSpec((tq, LANES), lambda b, h, i: (i, col(b, h, i))),
                  pl.BlockSpec((tq, LANES), lambda b, h, i: (jnp.minimum(i + 1, nt - 1), col(b, h, i))),
                  pl.BlockSpec((s_all, LANES), lambda b, h, i: (0, col(b, h, i))),
                  pl.BlockSpec((s_all, LANES), lambda b, h, i: (0, col(b, h, i)))],
        out_specs=pl.BlockSpec((tq, LANES), lambda b, h, i: (i, col(b, h, i))),
        out_shape=jax.ShapeDtypeStruct((s_all, batch * width), BF16),
        scratch_shapes=[pltpu.VMEM((2 * tq, s_all), F32), pltpu.VMEM((2 * tq, s_all), F32),
                        pltpu.VMEM((2 * tq, LANES), F32), pltpu.VMEM((2 * tq, LANES), F32),
                        pltpu.VMEM((s_all, 2 * ATT_V_DIM), BF16)],
        compiler_params=_cparams(("parallel", "parallel", "arbitrary")),
        name="diff_attention",
    )(lam_qk, subln_g, view(q), view(q), view(k), view(v))
    return out.reshape(rows, width)


def _scan_chunk(d, j, n_ctx_chunks, nch):
    back = jnp.where(j < n_ctx_chunks, n_ctx_chunks - 1 - j, nch - 1 + n_ctx_chunks - j)
    return jnp.where(d == 0, j, back)


def _lru_kernel(xp_ref, xc_ref, xn_ref, cw_ref, cb_ref, wg_ref, bg_ref, lam_ref, o_ref,
                a_scr, b_scr, h_scr, *, n_ctx_chunks, nch):
    d = pl.program_id(0)
    j = pl.program_id(1)
    c = _scan_chunk(d, j, n_ctx_chunks, nch)
    tm = xc_ref.shape[0]
    first = jnp.logical_or(c == 0, c == n_ctx_chunks)
    last = jnp.logical_or(c == n_ctx_chunks - 1, c == nch - 1)
    xp = xp_ref[...].astype(F32) * jnp.where(first, 0.0, 1.0)
    xn = xn_ref[...].astype(F32)[:SUBLANES] * jnp.where(last, 0.0, 1.0)
    xe = jnp.concatenate([xp, xc_ref[...].astype(F32), xn], axis=0)
    u = cb_ref[...]
    for tap in range(CONV_W):
        u = u + cw_ref[tap:tap + 1, :] * xe[tap * SUBLANES:tap * SUBLANES + tm]
    gates = jnp.tanh(jnp.dot(u.astype(BF16), wg_ref[...], preferred_element_type=F32) + bg_ref[...])
    z = -lam_ref[...]
    softplus = jnp.maximum(z, 0.0) + jnp.log(1.0 + jnp.exp(-jnp.abs(z)))
    half_c = (-0.5 * LRU_C * LOG2E) * softplus
    a = jnp.exp2(half_c + half_c * gates[:, :BRANCH_W])
    half_u = 0.5 * u
    a_scr[...] = a
    b_scr[...] = jnp.exp2(0.5 * jnp.log2(1.0 - a * a)) * (half_u + half_u * gates[:, BRANCH_W:])

    @pl.when(j == 0)
    def _():
        h_scr[...] = jnp.zeros(h_scr.shape, F32)

    nt = tm // SUBLANES

    def step(t, h):
        te = jnp.where(d == 0, t, nt - 1 - t)
        r0 = pl.multiple_of(te * SUBLANES, SUBLANES)
        h = a_scr[pl.ds(r0, SUBLANES), :] * h + b_scr[pl.ds(r0, SUBLANES), :]
        o_ref[pl.ds(r0, SUBLANES), :] = h
        return h

    h_scr[...] = lax.fori_loop(0, nt, step, h_scr[...], unroll=8)


def _rglru(xr, conv_w, conv_b, w_gate, b_gate, lru_lam, n_ctx_chunks):
    rows, width = xr.shape
    tm = CHUNK_T * SUBLANES
    nch = rows // tm
    halo = 2 * SUBLANES
    per = tm // halo
    chunk = lambda d, j: _scan_chunk(d, j, n_ctx_chunks, nch)
    return pl.pallas_call(
        functools.partial(_lru_kernel, n_ctx_chunks=n_ctx_chunks, nch=nch),
        grid=(2, nch),
        in_specs=[pl.BlockSpec((halo, width), lambda d, j: (jnp.maximum(chunk(d, j) * per - 1, 0), 0)),
                  pl.BlockSpec((tm, width), lambda d, j: (chunk(d, j), 0)),
                  pl.BlockSpec((halo, width),
                               lambda d, j: (jnp.minimum((chunk(d, j) + 1) * per, nch * per - 1), 0)),
                  pl.BlockSpec((CONV_W, width), lambda d, j: (0, 0)),
                  pl.BlockSpec((1, width), lambda d, j: (0, 0)),
                  pl.BlockSpec((None, width, 2 * width), lambda d, j: (d, 0, 0)),
                  pl.BlockSpec((None, 1, 2 * width), lambda d, j: (d, 0, 0)),
                  pl.BlockSpec((None, 1, width), lambda d, j: (d, 0, 0))],
        out_specs=pl.BlockSpec((None, tm, width), lambda d, j: (d, chunk(d, j), 0)),
        out_shape=jax.ShapeDtypeStruct((2, rows, width), F32),
        scratch_shapes=[pltpu.VMEM((tm, width), F32), pltpu.VMEM((tm, width), F32),
                        pltpu.VMEM((SUBLANES, width), F32)],
        compiler_params=_cparams(("arbitrary", "arbitrary")),
        name="rglru",
    )(xr, xr, xr, conv_w, conv_b, w_gate, b_gate, lru_lam)


def _s5_kernel(us_ref, bd_ref, cd_ref, ar_ref, ai_ref, o_ref, drv_scr, h_scr):
    d = pl.program_id(0)
    j = pl.program_id(1)
    tm = us_ref.shape[0]
    nt = tm // SUBLANES
    bw = 2 * S5_HALF

    @pl.when(j == 0)
    def _():
        h_scr[...] = jnp.zeros(h_scr.shape, F32)

    def drive(jj):
        drv_scr[:, jj * bw:(jj + 1) * bw] = jnp.dot(us_ref[:, jj * LANES:(jj + 1) * LANES], bd_ref[jj],
                                                    preferred_element_type=F32)

    def scan(jj, order):
        re = slice(jj * bw, jj * bw + S5_HALF)
        im = slice(jj * bw + S5_HALF, (jj + 1) * bw)
        ar = jnp.broadcast_to(ar_ref[jj], (SUBLANES, S5_HALF))
        ai = jnp.broadcast_to(ai_ref[jj], (SUBLANES, S5_HALF))
        hr = h_scr[:, re]
        hi = h_scr[:, im]
        for t in order:
            rws = slice(t * SUBLANES, (t + 1) * SUBLANES)
            hr, hi = (ar * hr - ai * hi + drv_scr[rws, re], ar * hi + ai * hr + drv_scr[rws, im])
            drv_scr[rws, re] = hr
            drv_scr[rws, im] = hi
        h_scr[:, re] = hr
        h_scr[:, im] = hi

    def readout(jj):
        o_ref[:, jj * LANES:(jj + 1) * LANES] = jnp.dot(drv_scr[:, jj * bw:(jj + 1) * bw].astype(BF16),
                                                        cd_ref[jj], preferred_element_type=F32)

    def run(order):
        for stage in range(S5_BLOCKS + 2):
            if stage < S5_BLOCKS:
                drive(stage)
            if 1 <= stage <= S5_BLOCKS:
                scan(stage - 1, order)
            if stage >= 2:
                readout(stage - 2)

    @pl.when(d == 0)
    def _():
        run(range(nt))

    @pl.when(d == 1)
    def _():
        run(range(nt - 1, -1, -1))


def _s5(us, bd, cd, ar, ai, n_ctx_chunks):
    rows, width = us.shape
    tm = CHUNK_T * SUBLANES
    nch = rows // tm
    chunk = lambda d, j: _scan_chunk(d, j, n_ctx_chunks, nch)
    return pl.pallas_call(
        _s5_kernel,
        grid=(2, nch),
        in_specs=[pl.BlockSpec((tm, width), lambda d, j: (chunk(d, j), 0)),
                  pl.BlockSpec((None,) + bd.shape[1:], lambda d, j: (d, 0, 0, 0)),
                  pl.BlockSpec(cd.shape, lambda d, j: (0, 0, 0)),
                  pl.BlockSpec((None,) + ar.shape[1:], lambda d, j: (d, 0, 0, 0)),
                  pl.BlockSpec((None,) + ai.shape[1:], lambda d, j: (d, 0, 0, 0))],
        out_specs=pl.BlockSpec((None, tm, width), lambda d, j: (d, chunk(d, j), 0)),
        out_shape=jax.ShapeDtypeStruct((2, rows, width), F32),
        scratch_shapes=[pltpu.VMEM((tm, 2 * S5_LANES), F32),
                        pltpu.VMEM((SUBLANES, 2 * S5_LANES), F32)],
        compiler_params=_cparams(("arbitrary", "arbitrary")),
        name="s5_scan",
    )(us, bd, cd, ar, ai)


def _s5_disc_kernel(lr_ref, li_ref, ldt_ref, ar_ref, ai_ref, cr_ref, ci_ref):
    lr = lr_ref[...]
    li = li_ref[...]
    dt = jnp.exp(ldt_ref[...])
    mag = jnp.exp(lr * dt)
    ar = mag * jnp.cos(li * dt)
    ai = mag * jnp.sin(li * dt)
    den = lr * lr + li * li
    nr = ar - 1.0
    ar_ref[...] = ar
    ai_ref[...] = ai
    cr_ref[...] = (nr * lr + ai * li) / den
    ci_ref[...] = (ai * lr - nr * li) / den


def _s5_discretise(lam_re, lam_im, log_dt):
    n = lam_re.shape[0]
    shp = jax.ShapeDtypeStruct((n, S5_STATE), F32)
    return pl.pallas_call(_s5_disc_kernel, out_shape=[shp] * 4, name="s5_discretise")(
        lam_re, lam_im, jnp.broadcast_to(log_dt, (n, S5_STATE)))


def _gelu_tanh(y):
    return 0.5 * y * (1.0 + jnp.tanh(math.sqrt(2.0 / math.pi) * (y + 0.044715 * (y * y * y))))


def _merge_kernel(x_ref, mod_ref, ya_ref, hl_ref, ys_ref, us_ref, za_ref, zr_ref, zs_ref, gl_ref,
                  sd_ref, wglu_ref, bglu_ref, wbr_ref, wout_ref, o_ref):
    tm, d = x_ref.shape
    y = ys_ref[0] + ys_ref[1] + sd_ref[...] * us_ref[...].astype(F32)
    g = _gelu_tanh(y)
    hg = 0.5 * g
    ys = hg + hg * jnp.tanh(jnp.dot(g.astype(BF16), wglu_ref[...], preferred_element_type=F32)
                            + bglu_ref[...])
    yr = hl_ref[0] + hl_ref[1]
    ya = ya_ref[...].astype(F32)
    acc = jnp.zeros((tm, d), F32)
    for n, (yn, zn_ref) in enumerate(((ya, za_ref), (yr, zr_ref), (ys, zs_ref))):
        zh = zn_ref[...].astype(F32)
        yz = yn * zh
        hyb = jnp.dot((yz + yz * jnp.tanh(zh)).astype(BF16), wbr_ref[n], preferred_element_type=F32)
        acc = acc + (hyb + hyb * jnp.tanh(gl_ref[:, n * d:(n + 1) * d].astype(F32)))
    upd = jnp.dot(acc.astype(BF16), wout_ref[...], preferred_element_type=F32)
    split = lambda a: a.reshape(tm // SUBLANES, SUBLANES, d)
    o_ref[...] = (split(x_ref[...]) + mod_ref[2][None] * split(upd)).reshape(tm, d)


def _merge(x, mod, ya, hl, ys, us, za, zr, zs, gl, s5_d, w_glu, b_glu, w_branch, w_out, n_ctx_chunks):
    rows, d = x.shape
    tm = CHUNK_T * SUBLANES
    nch = rows // tm
    w = BRANCH_W
    row_spec = lambda n: pl.BlockSpec((tm, n), lambda i: (i, 0))
    pair_spec = pl.BlockSpec((2, tm, w), lambda i: (0, i, 0))
    full = lambda a: pl.BlockSpec(a.shape, lambda i: (0,) * a.ndim)
    return pl.pallas_call(
        _merge_kernel,
        grid=(nch,),
        in_specs=[row_spec(d),
                  pl.BlockSpec((None, 3, SUBLANES, d),
                               lambda i: (jnp.where(i < n_ctx_chunks, 0, 1), 0, 0, 0)),
                  row_spec(w), pair_spec, pair_spec, row_spec(w), row_spec(w), row_spec(w),
                  row_spec(w), row_spec(N_BRANCH * d),
                  full(s5_d), full(w_glu), full(b_glu), full(w_branch), full(w_out)],
        out_specs=row_spec(d),
        out_shape=jax.ShapeDtypeStruct((rows, d), F32),
        compiler_params=_cparams(("parallel",)),
        name="gated_merge",
    )(x, mod, ya, hl, ys, us, za, zr, zs, gl, s5_d, w_glu, b_glu, w_branch, w_out)


def _final_kernel(x_ref, g_ref, o_ref):
    x = x_ref[...]
    o_ref[...] = x * lax.rsqrt(jnp.mean(x * x, axis=-1, keepdims=True) + EPS) * g_ref[...]


def _final_norm(x, g, n_ctx_chunks):
    rows, d = x.shape
    tm = CHUNK_T * SUBLANES
    nlat = rows // tm - n_ctx_chunks
    return pl.pallas_call(
        _final_kernel,
        grid=(nlat,),
        in_specs=[pl.BlockSpec((tm, d), lambda i: (i + n_ctx_chunks, 0)),
                  pl.BlockSpec((1, d), lambda i: (0, 0))],
        out_specs=pl.BlockSpec((tm, d), lambda i: (i, 0)),
        out_shape=jax.ShapeDtypeStruct((nlat * tm, d), F32),
        compiler_params=_cparams(("parallel",)),
        name="final_norm",
    )(x, g)


def _rope_tables(n_lat, n_ctx, batch):
    n_freq = ATT_QK_DIM // 4
    tl = jnp.arange(n_lat)
    inv = ROPE_BASE ** (-jnp.arange(n_freq, dtype=F32) / n_freq)
    ang = jnp.concatenate([(tl // GRID_W).astype(F32)[:, None] * inv,
                           (tl % GRID_W).astype(F32)[:, None] * inv], axis=-1)
    cos = jnp.tile(jnp.cos(ang), (1, 4))
    sin = jnp.tile(jnp.sin(ang), (1, 4)) * jnp.where(jnp.arange(LANES) < LANES // 2, -1.0, 1.0)
    cos = jnp.concatenate([jnp.ones((n_ctx, LANES), F32), cos], axis=0)
    sin = jnp.concatenate([jnp.zeros((n_ctx, LANES), F32), sin], axis=0)
    rep = lambda a: jnp.broadcast_to(a[:, None, :], (a.shape[0], batch, LANES)).reshape(-1, LANES)
    return rep(cos), rep(sin)


def _rope_col_perm():
    j = jnp.arange(LANES)
    within = ((j // 32) % 2) * ATT_QK_DIM + 2 * (j % 32) + j // 64
    return (jnp.arange(ATT_HEADS)[:, None] * LANES + within[None, :]).reshape(-1)


def _block_diag(w):
    *lead, n, c, _ = w.shape
    dense = w[..., :, :, None, :] * jnp.eye(n, dtype=w.dtype)[:, None, :, None]
    return dense.reshape(*lead, n * c, n * c)


def kernel(x, c, ctx, c_ctx, w_mod, b_mod, norm_g, w_in, lam_qk, subln_g, conv_w, conv_b, lru_wa, lru_ba,
           lru_wx, lru_bx, lru_lam, s5_lam_re, s5_lam_im, s5_log_dt, s5_b_re, s5_b_im, s5_c_re, s5_c_im,
           s5_d, s5_w_glu, s5_b_glu, w_branch, w_out, final_g):
    batch, n_lat, d = x.shape
    n_ctx = ctx.shape[1]
    depth = w_mod.shape[0]
    assert batch == SUBLANES and n_lat % CHUNK_T == 0 and n_ctx % CHUNK_T == 0
    n_ctx_chunks = n_ctx // CHUNK_T
    s_all = n_ctx + n_lat
    rows = s_all * batch

    xs = jnp.concatenate([jnp.swapaxes(ctx, 0, 1), jnp.swapaxes(x, 0, 1)], axis=0).reshape(rows, d)

    cond = jnp.zeros((2 * SUBLANES, d), F32).at[:batch].set(c).at[batch].set(c_ctx)
    m = _ada_mod(cond, w_mod, b_mod).reshape(depth, 2 * SUBLANES, 3, d)
    mod = jnp.stack([jnp.broadcast_to(m[:, batch][:, None], (depth, batch, 3, d)), m[:, :batch]], axis=1)
    mod = jnp.swapaxes(mod, 2, 3)

    cos, sin = _rope_tables(n_lat, n_ctx, batch)
    perm = _rope_col_perm()
    qk_w = ATT_HEADS * 2 * ATT_QK_DIM
    seg = jnp.arange(w_in.shape[-1]) // BRANCH_W
    halved = (seg == 3) | (seg == 5) | (seg >= 7)
    w_in_p = jnp.concatenate([w_in[:, :, :qk_w][:, :, perm], w_in[:, :, qk_w:2 * qk_w][:, :, perm],
                              w_in[:, :, 2 * qk_w:]], axis=-1)
    w_in_p = (w_in_p * jnp.where(halved, 0.5, 1.0)).astype(BF16)

    n_dir = depth * 2 * S5_GROUPS
    ar, ai, cr, ci = _s5_discretise(s5_lam_re.reshape(n_dir, S5_STATE), s5_lam_im.reshape(n_dir, S5_STATE),
                                    s5_log_dt.reshape(n_dir, 1))
    gshape = (depth, 2, S5_GROUPS, S5_STATE)
    ar, ai, cr, ci = (a.reshape(gshape) for a in (ar, ai, cr, ci))
    bbr = cr[..., None] * s5_b_re - ci[..., None] * s5_b_im
    bbi = cr[..., None] * s5_b_im + ci[..., None] * s5_b_re
    nb, bg = S5_BLOCKS, S5_BLOCK_GROUPS
    eye_g = jnp.eye(bg, dtype=F32)

    def drive(bb):
        bb = bb.reshape(depth, 2, nb, bg, S5_STATE, S5_GROUP)
        return (bb.transpose(0, 1, 2, 3, 5, 4)[:, :, :, :, :, None, :]
                * eye_g[:, None, :, None]).reshape(depth, 2, nb, bg * S5_GROUP, S5_HALF)

    def read(cc):
        cc = cc.reshape(depth, nb, bg, S5_GROUP, S5_STATE)
        return (cc.transpose(0, 1, 2, 4, 3)[:, :, :, :, None, :]
                * eye_g[:, None, :, None]).reshape(depth, nb, S5_HALF, bg * S5_GROUP)

    bd = jnp.concatenate([drive(bbr), drive(bbi)], axis=-1).astype(BF16)
    cd = jnp.concatenate([read(s5_c_re), -read(s5_c_im)], axis=2).astype(BF16)
    ar = ar.reshape(depth, 2, nb, 1, S5_HALF)
    ai = ai.reshape(depth, 2, nb, 1, S5_HALF)

    w_gate = (0.5 * jnp.concatenate([_block_diag(lru_wa), _block_diag(lru_wx)], axis=-1)).astype(BF16)
    b_gate = 0.5 * jnp.concatenate([lru_ba, lru_bx], axis=-1)[:, :, None, :]
    w_glu_h = (0.5 * s5_w_glu).astype(BF16)
    b_glu_h = 0.5 * s5_b_glu
    w_branch_h = (0.5 * w_branch).astype(BF16)
    w_out_b = w_out.astype(BF16)

    for l in range(depth):
        lam_init = 0.8 - 0.6 * math.exp(-0.3 * l)
        q, k, v, za, xr, zr, us, zs, gl = _in_proj(xs, norm_g[l][None], mod[l], cos, sin, w_in_p[l],
                                                   n_ctx_chunks)
        ya = _attention(q, k, v, lam_qk[l], subln_g[l][None], lam_init, batch, n_ctx)
        hl = _rglru(xr, conv_w[l], conv_b[l][None], w_gate[l], b_gate[l], lru_lam[l][:, None, :],
                    n_ctx_chunks)
        ys = _s5(us, bd[l], cd[l], ar[l], ai[l], n_ctx_chunks)
        xs = _merge(xs, mod[l], ya, hl, ys, us, za, zr, zs, gl, s5_d[l][None], w_glu_h[l],
                    b_glu_h[l][None], w_branch_h[l], w_out_b[l], n_ctx_chunks)

    out = _final_norm(xs, final_g[None], n_ctx_chunks)
    return jnp.swapaxes(out.reshape(n_lat, batch, d), 0, 1)
```

```python
import functools
import math

import jax
import jax.numpy as jnp
from jax import lax
from jax.experimental import pallas as pl
from jax.experimental.pallas import tpu as pltpu

F32 = jnp.float32
BF16 = jnp.bfloat16

EPS = 1e-6
GRID_W = 64
ATT_HEADS = 4
ATT_QK_DIM = 64
ATT_V_DIM = 2 * ATT_QK_DIM
BRANCH_W = 512
N_BRANCH = 3
ROPE_BASE = 10000.0
LRU_BLOCKS = 8
LRU_C = 8.0
CONV_W = 4
S5_GROUP = 16
S5_GROUPS = BRANCH_W // S5_GROUP
S5_STATE = 64
S5_LANES = S5_GROUPS * S5_STATE
S5_BLOCK_GROUPS = 8
S5_BLOCKS = S5_GROUPS // S5_BLOCK_GROUPS
S5_HALF = S5_BLOCK_GROUPS * S5_STATE

SUBLANES = 8
LANES = 128
CHUNK_T = 64
KEY_CHUNK = 256
SCORE_LAG = 2
VMEM_LIMIT = 56 * 1024 * 1024

LOG2E = math.log2(math.e)


def _cparams(sem):
    return pltpu.CompilerParams(dimension_semantics=sem, vmem_limit_bytes=VMEM_LIMIT)


def _sigmoid(z):
    return 1.0 / (1.0 + jnp.exp(-z))


def _silu(z):
    return z * _sigmoid(z)


def _mod_kernel(c_ref, w_ref, b_ref, o_ref):
    c = c_ref[...]
    o_ref[...] = jnp.dot(_silu(c), w_ref[...], preferred_element_type=F32,
                         precision=lax.Precision.HIGHEST) + b_ref[...]


def _ada_mod(cond, w_mod, b_mod):
    depth, d, d3 = w_mod.shape
    return pl.pallas_call(
        _mod_kernel,
        grid=(depth, d3 // d),
        in_specs=[pl.BlockSpec((2 * SUBLANES, d), lambda l, j: (0, 0)),
                  pl.BlockSpec((None, d, d), lambda l, j: (l, 0, j)),
                  pl.BlockSpec((None, 1, d), lambda l, j: (l, 0, j))],
        out_specs=pl.BlockSpec((None, 2 * SUBLANES, d), lambda l, j: (l, 0, j)),
        out_shape=jax.ShapeDtypeStruct((depth, 2 * SUBLANES, d3), F32),
        compiler_params=_cparams(("parallel", "parallel")),
        name="ada_mod",
    )(cond, w_mod, b_mod.reshape(depth, 1, d3))


def _inproj_kernel(x_ref, g_ref, mod_ref, cos_ref, sin_ref, w_ref,
                   q_ref, k_ref, v_ref, za_ref, xr_ref, zr_ref, us_ref, zs_ref, gl_ref,
                   h_scr, slab_scr, *, q_scale):
    x = x_ref[...]
    tm, d = x.shape
    nt = tm // SUBLANES
    y = x * lax.rsqrt(jnp.mean(x * x, axis=-1, keepdims=True) + EPS) * g_ref[...]
    y = y.reshape(nt, SUBLANES, d)
    h = y * (1.0 + mod_ref[1])[None] + mod_ref[0][None]
    h_scr[...] = h.reshape(tm, d).astype(BF16)

    def proj(j):
        return jnp.dot(h_scr[...], w_ref[:, j * BRANCH_W:(j + 1) * BRANCH_W],
                       preferred_element_type=F32)

    cos = cos_ref[...]
    sin = sin_ref[...]

    def to_batch_major(a, ref, slab0, rope, scale):
        for hh in range(ATT_HEADS):
            r = a[:, hh * LANES:(hh + 1) * LANES]
            if rope:
                r = (r * cos + pltpu.roll(r, LANES // 2, axis=1) * sin) * scale
            slab_scr[slab0 + hh] = r
            for b in range(SUBLANES):
                ref[b, :, hh * LANES:(hh + 1) * LANES] = (
                    slab_scr[slab0 + hh, pl.ds(b, nt, stride=SUBLANES), :].astype(BF16))

    to_batch_major(proj(0), q_ref, 0, True, q_scale)
    to_batch_major(proj(1), k_ref, ATT_HEADS, True, 1.0)
    to_batch_major(proj(2), v_ref, 2 * ATT_HEADS, False, 1.0)
    for j, ref in ((3, za_ref), (4, xr_ref), (5, zr_ref), (6, us_ref), (7, zs_ref)):
        ref[...] = proj(j).astype(BF16)
    for j in range(2 * N_BRANCH):
        gl_ref[:, j * BRANCH_W:(j + 1) * BRANCH_W] = proj(8 + j).astype(BF16)


def _in_proj(x, norm_g, mod, cos, sin, w_in, n_ctx_chunks):
    rows, d = x.shape
    tm = CHUNK_T * SUBLANES
    nch = rows // tm
    row_spec = lambda w: pl.BlockSpec((tm, w), lambda i: (i, 0))
    out_w = [BRANCH_W] * 5 + [N_BRANCH * d]
    bm_spec = pl.BlockSpec((SUBLANES, CHUNK_T, BRANCH_W), lambda i: (0, i, 0))
    bm_shape = jax.ShapeDtypeStruct((SUBLANES, rows // SUBLANES, BRANCH_W), BF16)
    return pl.pallas_call(
        functools.partial(_inproj_kernel, q_scale=ATT_QK_DIM ** -0.5 * LOG2E),
        grid=(nch,),
        in_specs=[row_spec(d),
                  pl.BlockSpec((1, d), lambda i: (0, 0)),
                  pl.BlockSpec((None, 3, SUBLANES, d),
                               lambda i: (jnp.where(i < n_ctx_chunks, 0, 1), 0, 0, 0)),
                  row_spec(LANES), row_spec(LANES),
                  pl.BlockSpec(w_in.shape, lambda i: (0, 0), pipeline_mode=pl.Buffered(1))],
        out_specs=[bm_spec] * 3 + [row_spec(w) for w in out_w],
        out_shape=[bm_shape] * 3 + [jax.ShapeDtypeStruct((rows, w), BF16) for w in out_w],
        scratch_shapes=[pltpu.VMEM((tm, d), BF16), pltpu.VMEM((3 * ATT_HEADS, tm, LANES), F32)],
        compiler_params=_cparams(("parallel",)),
        name="in_proj",
    )(x, norm_g, mod, cos, sin, w_in)


def _attn_kernel(lamqk_ref, g_ref, qc_ref, qn_ref, k_ref, v_ref, o_ref,
                 s0_scr, s1_scr, m0_scr, m1_scr, vext_scr, *, tq, n_ctx_tiles, n_ctx, n_all, lam_init):
    i = pl.program_id(2)

    @pl.when(i == 0)
    def _():
        for off in range(0, n_all, LANES):
            vext_scr[:ATT_V_DIM, off:off + LANES] = v_ref[off:off + LANES, :].astype(F32).T.astype(BF16)
        vext_scr[ATT_V_DIM:, :] = jnp.ones((2 * SUBLANES, n_all), BF16)

    lq = lamqk_ref[...]
    lam = (jnp.exp(jnp.sum(lq[0:1] * lq[1:2], axis=-1, keepdims=True))
           - jnp.exp(jnp.sum(lq[2:3] * lq[3:4], axis=-1, keepdims=True)) + lam_init)

    def chunks(nk):
        return [(off, min(KEY_CHUNK, nk - off)) for off in range(0, nk, KEY_CHUNK)]

    cols = (slice(0, tq), slice(tq, 2 * tq))

    def split(q_ref):
        q = q_ref[...]
        lane = lax.broadcasted_iota(jnp.int32, q.shape, 1)
        comp0 = (lane % (LANES // 2)) < (LANES // 4)
        zero = jnp.zeros_like(q)
        return jnp.where(comp0, q, zero), jnp.where(comp0, zero, q)

    def score_chunk(qs, c, off, kc, s_dst, mm):
        st = lax.dot_general(k_ref[off:off + kc, :], qs[c], (((1,), (1,)), ((), ())),
                             preferred_element_type=F32)
        s_dst[off:off + kc, cols[c]] = st
        part = jnp.max(st.reshape(kc // SUBLANES, SUBLANES, tq), axis=0)
        return part if mm is None else jnp.maximum(mm, part)

    def score_done(c, m_dst, mm):
        m_dst[:, cols[c]] = jnp.broadcast_to(jnp.max(mm, axis=0, keepdims=True), (SUBLANES, tq))

    def weight_chunk(c, off, kc, s_src, m, acc):
        st = s_src[off:off + kc, cols[c]].reshape(kc // SUBLANES, SUBLANES, tq)
        pt = jnp.exp2(st - m[None]).reshape(kc, tq).astype(BF16)
        return acc + jnp.dot(vext_scr[:, off:off + kc], pt, preferred_element_type=F32)

    def weight_done(accs):
        ots = [a[:ATT_V_DIM] / a[ATT_V_DIM:ATT_V_DIM + 1] for a in accs]
        o = (ots[0] - lam * ots[1]).T
        o = o * lax.rsqrt(jnp.mean(o * o, axis=-1, keepdims=True) + EPS) * g_ref[...]
        o_ref[...] = (o * (1.0 - lam_init)).astype(BF16)

    acc0 = jnp.zeros((ATT_V_DIM + 2 * SUBLANES, tq), F32)

    def scores(q_ref, nk, s_dst, m_dst):
        qs = split(q_ref)
        for c in range(2):
            mm = None
            for off, kc in chunks(nk):
                mm = score_chunk(qs, c, off, kc, s_dst, mm)
            score_done(c, m_dst, mm)

    def weighted(nk, s_src, m_src):
        accs = []
        for c in range(2):
            acc = acc0
            for off, kc in chunks(nk):
                acc = weight_chunk(c, off, kc, s_src, m_src[:, cols[c]], acc)
            accs.append(acc)
        weight_done(accs)

    def both(nk, s_src, m_src, q_ref, s_dst, m_dst):
        qs = split(q_ref)
        ms = [m_src[:, cols[c]] for c in range(2)]
        accs = [acc0, acc0]
        mms = [None, None]
        ch = chunks(nk)
        for j in range(len(ch) + SCORE_LAG):
            if j < len(ch):
                for c in range(2):
                    accs[c] = weight_chunk(c, *ch[j], s_src, ms[c], accs[c])
                if j == len(ch) - 1:
                    weight_done(accs)
            if j >= SCORE_LAG:
                for c in range(2):
                    mms[c] = score_chunk(qs, c, *ch[j - SCORE_LAG], s_dst, mms[c])
        for c in range(2):
            score_done(c, m_dst, mms[c])

    slots = ((s0_scr, m0_scr), (s1_scr, m1_scr))
    for par in range(2):
        cur, nxt = slots[par], slots[1 - par]
        mine = i % 2 == par

        @pl.when(jnp.logical_and(mine, i < n_ctx_tiles))
        def _(cur=cur):
            scores(qc_ref, n_ctx, *cur)
            weighted(n_ctx, *cur)

        @pl.when(jnp.logical_and(mine, i == n_ctx_tiles - 1))
        def _(nxt=nxt):
            scores(qn_ref, n_all, *nxt)

        @pl.when(jnp.logical_and(mine, i >= n_ctx_tiles))
        def _(cur=cur, nxt=nxt):
            both(n_all, *cur, qn_ref, *nxt)


def _attention(q, k, v, lam_qk, subln_g, lam_init, n_ctx):
    batch, s_all, width = q.shape
    tq = min(256, n_ctx)
    nt = s_all // tq
    return pl.pallas_call(
        functools.partial(_attn_kernel, tq=tq, n_ctx_tiles=n_ctx // tq, n_ctx=n_ctx, n_all=s_all,
                          lam_init=lam_init),
        grid=(batch, ATT_HEADS, nt),
        in_specs=[pl.BlockSpec((4, ATT_QK_DIM), lambda b, h, i: (0, 0)),
                  pl.BlockSpec((1, ATT_V_DIM), lambda b, h, i: (0, 0)),
                  pl.BlockSpec((None, tq, LANES), lambda b, h, i: (b, i, h)),
                  pl.BlockSpec((None, tq, LANES), lambda b, h, i: (b, jnp.minimum(i + 1, nt - 1), h)),
                  pl.BlockSpec((None, s_all, LANES), lambda b, h, i: (b, 0, h)),
                  pl.BlockSpec((None, s_all, LANES), lambda b, h, i: (b, 0, h))],
        out_specs=pl.BlockSpec((None, tq, LANES), lambda b, h, i: (b, i, h)),
        out_shape=jax.ShapeDtypeStruct((batch, s_all, width), BF16),
        scratch_shapes=[pltpu.VMEM((s_all, 2 * tq), F32), pltpu.VMEM((s_all, 2 * tq), F32),
                        pltpu.VMEM((SUBLANES, 2 * tq), F32), pltpu.VMEM((SUBLANES, 2 * tq), F32),
                        pltpu.VMEM((ATT_V_DIM + 2 * SUBLANES, s_all), BF16)],
        compiler_params=_cparams(("parallel", "parallel", "arbitrary")),
        name="diff_attention",
    )(lam_qk, subln_g, q, q, k, v)


def _scan_chunk(d, j, n_ctx_chunks, nch):
    back = jnp.where(j < n_ctx_chunks, n_ctx_chunks - 1 - j, nch - 1 + n_ctx_chunks - j)
    return jnp.where(d == 0, j, back)


def _lru_kernel(xp_ref, xc_ref, xn_ref, cw_ref, cb_ref, wg_ref, bg_ref, lam_ref, o_ref,
                a_scr, b_scr, h_scr, *, n_ctx_chunks, nch):
    d = pl.program_id(0)
    j = pl.program_id(1)
    c = _scan_chunk(d, j, n_ctx_chunks, nch)
    tm = xc_ref.shape[0]
    first = jnp.logical_or(c == 0, c == n_ctx_chunks)
    last = jnp.logical_or(c == n_ctx_chunks - 1, c == nch - 1)
    xp = xp_ref[...].astype(F32) * jnp.where(first, 0.0, 1.0)
    xn = xn_ref[...].astype(F32)[:SUBLANES] * jnp.where(last, 0.0, 1.0)
    xe = jnp.concatenate([xp, xc_ref[...].astype(F32), xn], axis=0)
    u = cb_ref[...]
    for tap in range(CONV_W):
        u = u + cw_ref[tap:tap + 1, :] * xe[tap * SUBLANES:tap * SUBLANES + tm]
    gates = jnp.tanh(jnp.dot(u.astype(BF16), wg_ref[...], preferred_element_type=F32) + bg_ref[...])
    z = -lam_ref[...]
    softplus = jnp.maximum(z, 0.0) + jnp.log(1.0 + jnp.exp(-jnp.abs(z)))
    half_c = (-0.5 * LRU_C * LOG2E) * softplus
    a = jnp.exp2(half_c + half_c * gates[:, :BRANCH_W])
    half_u = 0.5 * u
    a_scr[...] = a
    b_scr[...] = jnp.exp2(0.5 * jnp.log2(1.0 - a * a)) * (half_u + half_u * gates[:, BRANCH_W:])

    @pl.when(j == 0)
    def _():
        h_scr[...] = jnp.zeros(h_scr.shape, F32)

    nt = tm // SUBLANES

    def step(t, h):
        te = jnp.where(d == 0, t, nt - 1 - t)
        r0 = pl.multiple_of(te * SUBLANES, SUBLANES)
        h = a_scr[pl.ds(r0, SUBLANES), :] * h + b_scr[pl.ds(r0, SUBLANES), :]
        o_ref[pl.ds(r0, SUBLANES), :] = h
        return h

    h_scr[...] = lax.fori_loop(0, nt, step, h_scr[...], unroll=8)


def _rglru(xr, conv_w, conv_b, w_gate, b_gate, lru_lam, n_ctx_chunks):
    rows, width = xr.shape
    tm = CHUNK_T * SUBLANES
    nch = rows // tm
    halo = 2 * SUBLANES
    per = tm // halo
    chunk = lambda d, j: _scan_chunk(d, j, n_ctx_chunks, nch)
    return pl.pallas_call(
        functools.partial(_lru_kernel, n_ctx_chunks=n_ctx_chunks, nch=nch),
        grid=(2, nch),
        in_specs=[pl.BlockSpec((halo, width), lambda d, j: (jnp.maximum(chunk(d, j) * per - 1, 0), 0)),
                  pl.BlockSpec((tm, width), lambda d, j: (chunk(d, j), 0)),
                  pl.BlockSpec((halo, width),
                               lambda d, j: (jnp.minimum((chunk(d, j) + 1) * per, nch * per - 1), 0)),
                  pl.BlockSpec((CONV_W, width), lambda d, j: (0, 0)),
                  pl.BlockSpec((1, width), lambda d, j: (0, 0)),
                  pl.BlockSpec((None, width, 2 * width), lambda d, j: (d, 0, 0)),
                  pl.BlockSpec((None, 1, 2 * width), lambda d, j: (d, 0, 0)),
                  pl.BlockSpec((None, 1, width), lambda d, j: (d, 0, 0))],
        out_specs=pl.BlockSpec((None, tm, width), lambda d, j: (d, chunk(d, j), 0)),
        out_shape=jax.ShapeDtypeStruct((2, rows, width), F32),
        scratch_shapes=[pltpu.VMEM((tm, width), F32), pltpu.VMEM((tm, width), F32),
                        pltpu.VMEM((SUBLANES, width), F32)],
        compiler_params=_cparams(("arbitrary", "arbitrary")),
        name="rglru",
    )(xr, xr, xr, conv_w, conv_b, w_gate, b_gate, lru_lam)


def _s5_kernel(us_ref, bd_ref, cd_ref, ar_ref, ai_ref, o_ref, drv_scr, h_scr):
    d = pl.program_id(0)
    j = pl.program_id(1)
    tm = us_ref.shape[0]
    nt = tm // SUBLANES
    bw = 2 * S5_HALF

    @pl.when(j == 0)
    def _():
        h_scr[...] = jnp.zeros(h_scr.shape, F32)

    def drive(jj):
        drv_scr[:, jj * bw:(jj + 1) * bw] = jnp.dot(us_ref[:, jj * LANES:(jj + 1) * LANES], bd_ref[jj],
                                                    preferred_element_type=F32)

    def scan(jj, order):
        re = slice(jj * bw, jj * bw + S5_HALF)
        im = slice(jj * bw + S5_HALF, (jj + 1) * bw)
        ar = jnp.broadcast_to(ar_ref[jj], (SUBLANES, S5_HALF))
        ai = jnp.broadcast_to(ai_ref[jj], (SUBLANES, S5_HALF))
        hr = h_scr[:, re]
        hi = h_scr[:, im]
        for t in order:
            rws = slice(t * SUBLANES, (t + 1) * SUBLANES)
            hr, hi = (ar * hr - ai * hi + drv_scr[rws, re], ar * hi + ai * hr + drv_scr[rws, im])
            drv_scr[rws, re] = hr
            drv_scr[rws, im] = hi
        h_scr[:, re] = hr
        h_scr[:, im] = hi

    def readout(jj):
        yt = lax.dot_general(cd_ref[jj], drv_scr[:, jj * bw:(jj + 1) * bw].astype(BF16),
                             (((1,), (1,)), ((), ())), preferred_element_type=F32)
        o_ref[:, jj * LANES:(jj + 1) * LANES] = yt.T

    def run(order):
        for stage in range(S5_BLOCKS + 2):
            if stage < S5_BLOCKS:
                drive(stage)
            if 1 <= stage <= S5_BLOCKS:
                scan(stage - 1, order)
            if stage >= 2:
                readout(stage - 2)

    @pl.when(d == 0)
    def _():
        run(range(nt))

    @pl.when(d == 1)
    def _():
        run(range(nt - 1, -1, -1))


def _s5(us, bd, cd, ar, ai, n_ctx_chunks):
    rows, width = us.shape
    tm = CHUNK_T * SUBLANES
    nch = rows // tm
    chunk = lambda d, j: _scan_chunk(d, j, n_ctx_chunks, nch)
    return pl.pallas_call(
        _s5_kernel,
        grid=(2, nch),
        in_specs=[pl.BlockSpec((tm, width), lambda d, j: (chunk(d, j), 0)),
                  pl.BlockSpec((None,) + bd.shape[1:], lambda d, j: (d, 0, 0, 0)),
                  pl.BlockSpec(cd.shape, lambda d, j: (0, 0, 0)),
                  pl.BlockSpec((None,) + ar.shape[1:], lambda d, j: (d, 0, 0, 0)),
                  pl.BlockSpec((None,) + ai.shape[1:], lambda d, j: (d, 0, 0, 0))],
        out_specs=pl.BlockSpec((None, tm, width), lambda d, j: (d, chunk(d, j), 0)),
        out_shape=jax.ShapeDtypeStruct((2, rows, width), F32),
        scratch_shapes=[pltpu.VMEM((tm, 2 * S5_LANES), F32),
                        pltpu.VMEM((SUBLANES, 2 * S5_LANES), F32)],
        compiler_params=_cparams(("arbitrary", "arbitrary")),
        name="s5_scan",
    )(us, bd, cd, ar, ai)


def _s5_disc_kernel(lr_ref, li_ref, ldt_ref, ar_ref, ai_ref, cr_ref, ci_ref):
    lr = lr_ref[...]
    li = li_ref[...]
    dt = jnp.exp(ldt_ref[...])
    mag = jnp.exp(lr * dt)
    ar = mag * jnp.cos(li * dt)
    ai = mag * jnp.sin(li * dt)
    den = lr * lr + li * li
    nr = ar - 1.0
    ar_ref[...] = ar
    ai_ref[...] = ai
    cr_ref[...] = (nr * lr + ai * li) / den
    ci_ref[...] = (ai * lr - nr * li) / den


def _s5_discretise(lam_re, lam_im, log_dt):
    n = lam_re.shape[0]
    shp = jax.ShapeDtypeStruct((n, S5_STATE), F32)
    return pl.pallas_call(_s5_disc_kernel, out_shape=[shp] * 4, name="s5_discretise")(
        lam_re, lam_im, jnp.broadcast_to(log_dt, (n, S5_STATE)))


def _gelu_tanh(y):
    return 0.5 * y * (1.0 + jnp.tanh(math.sqrt(2.0 / math.pi) * (y + 0.044715 * (y * y * y))))


def _merge_kernel(x_ref, mod_ref, ya_ref, hl_ref, ys_ref, us_ref, za_ref, zr_ref, zs_ref, gl_ref,
                  sd_ref, wglu_ref, bglu_ref, wbr_ref, wout_ref, o_ref, slab_scr):
    tm, d = x_ref.shape
    y = ys_ref[0] + ys_ref[1] + sd_ref[...] * us_ref[...].astype(F32)
    g = _gelu_tanh(y)
    hg = 0.5 * g
    ys = hg + hg * jnp.tanh(jnp.dot(g.astype(BF16), wglu_ref[...], preferred_element_type=F32)
                            + bglu_ref[...])
    yr = hl_ref[0] + hl_ref[1]
    for hh in range(ATT_HEADS):
        for b in range(SUBLANES):
            slab_scr[hh, pl.ds(b, tm // SUBLANES, stride=SUBLANES), :] = (
                ya_ref[b, :, hh * LANES:(hh + 1) * LANES].astype(F32))
    ya = jnp.concatenate([slab_scr[hh] for hh in range(ATT_HEADS)], axis=1)
    acc = jnp.zeros((tm, d), F32)
    for n, (yn, zn_ref) in enumerate(((ya, za_ref), (yr, zr_ref), (ys, zs_ref))):
        zh = zn_ref[...].astype(F32)
        yz = yn * zh
        hyb = jnp.dot((yz + yz * jnp.tanh(zh)).astype(BF16), wbr_ref[n], preferred_element_type=F32)
        acc = acc + (hyb + hyb * jnp.tanh(gl_ref[:, n * d:(n + 1) * d].astype(F32)))
    upd = jnp.dot(acc.astype(BF16), wout_ref[...], preferred_element_type=F32)
    split = lambda a: a.reshape(tm // SUBLANES, SUBLANES, d)
    o_ref[...] = (split(x_ref[...]) + mod_ref[2][None] * split(upd)).reshape(tm, d)


def _merge(x, mod, ya, hl, ys, us, za, zr, zs, gl, s5_d, w_glu, b_glu, w_branch, w_out, n_ctx_chunks):
    rows, d = x.shape
    tm = CHUNK_T * SUBLANES
    nch = rows // tm
    w = BRANCH_W
    row_spec = lambda n: pl.BlockSpec((tm, n), lambda i: (i, 0))
    pair_spec = pl.BlockSpec((2, tm, w), lambda i: (0, i, 0))
    full = lambda a: pl.BlockSpec(a.shape, lambda i: (0,) * a.ndim)
    return pl.pallas_call(
        _merge_kernel,
        grid=(nch,),
        in_specs=[row_spec(d),
                  pl.BlockSpec((None, 3, SUBLANES, d),
                               lambda i: (jnp.where(i < n_ctx_chunks, 0, 1), 0, 0, 0)),
                  pl.BlockSpec((SUBLANES, CHUNK_T, w), lambda i: (0, i, 0)),
                  pair_spec, pair_spec, row_spec(w), row_spec(w), row_spec(w),
                  row_spec(w), row_spec(N_BRANCH * d),
                  full(s5_d), full(w_glu), full(b_glu), full(w_branch), full(w_out)],
        out_specs=row_spec(d),
        out_shape=jax.ShapeDtypeStruct((rows, d), F32),
        scratch_shapes=[pltpu.VMEM((ATT_HEADS, tm, LANES), F32)],
        compiler_params=_cparams(("parallel",)),
        name="gated_merge",
    )(x, mod, ya, hl, ys, us, za, zr, zs, gl, s5_d, w_glu, b_glu, w_branch, w_out)


def _final_kernel(x_ref, g_ref, o_ref):
    x = x_ref[...]
    o_ref[...] = x * lax.rsqrt(jnp.mean(x * x, axis=-1, keepdims=True) + EPS) * g_ref[...]


def _final_norm(x, g, n_ctx_chunks):
    rows, d = x.shape
    tm = CHUNK_T * SUBLANES
    nlat = rows // tm - n_ctx_chunks
    return pl.pallas_call(
        _final_kernel,
        grid=(nlat,),
        in_specs=[pl.BlockSpec((tm, d), lambda i: (i + n_ctx_chunks, 0)),
                  pl.BlockSpec((1, d), lambda i: (0, 0))],
        out_specs=pl.BlockSpec((tm, d), lambda i: (i, 0)),
        out_shape=jax.ShapeDtypeStruct((nlat * tm, d), F32),
        compiler_params=_cparams(("parallel",)),
        name="final_norm",
    )(x, g)


def _rope_tables(n_lat, n_ctx, batch):
    n_freq = ATT_QK_DIM // 4
    tl = jnp.arange(n_lat)
    inv = ROPE_BASE ** (-jnp.arange(n_freq, dtype=F32) / n_freq)
    ang = jnp.concatenate([(tl // GRID_W).astype(F32)[:, None] * inv,
                           (tl % GRID_W).astype(F32)[:, None] * inv], axis=-1)
    cos = jnp.tile(jnp.cos(ang), (1, 4))
    sin = jnp.tile(jnp.sin(ang), (1, 4)) * jnp.where(jnp.arange(LANES) < LANES // 2, -1.0, 1.0)
    cos = jnp.concatenate([jnp.ones((n_ctx, LANES), F32), cos], axis=0)
    sin = jnp.concatenate([jnp.zeros((n_ctx, LANES), F32), sin], axis=0)
    rep = lambda a: jnp.broadcast_to(a[:, None, :], (a.shape[0], batch, LANES)).reshape(-1, LANES)
    return rep(cos), rep(sin)


def _rope_col_perm():
    j = jnp.arange(LANES)
    within = ((j // 32) % 2) * ATT_QK_DIM + 2 * (j % 32) + j // 64
    return (jnp.arange(ATT_HEADS)[:, None] * LANES + within[None, :]).reshape(-1)


def _block_diag(w):
    *lead, n, c, _ = w.shape
    dense = w[..., :, :, None, :] * jnp.eye(n, dtype=w.dtype)[:, None, :, None]
    return dense.reshape(*lead, n * c, n * c)


def kernel(x, c, ctx, c_ctx, w_mod, b_mod, norm_g, w_in, lam_qk, subln_g, conv_w, conv_b, lru_wa, lru_ba,
           lru_wx, lru_bx, lru_lam, s5_lam_re, s5_lam_im, s5_log_dt, s5_b_re, s5_b_im, s5_c_re, s5_c_im,
           s5_d, s5_w_glu, s5_b_glu, w_branch, w_out, final_g):
    batch, n_lat, d = x.shape
    n_ctx = ctx.shape[1]
    depth = w_mod.shape[0]
    assert batch == SUBLANES and n_lat % CHUNK_T == 0 and n_ctx % CHUNK_T == 0
    n_ctx_chunks = n_ctx // CHUNK_T
    s_all = n_ctx + n_lat
    rows = s_all * batch

    xs = jnp.concatenate([jnp.swapaxes(ctx, 0, 1), jnp.swapaxes(x, 0, 1)], axis=0).reshape(rows, d)

    cond = jnp.zeros((2 * SUBLANES, d), F32).at[:batch].set(c).at[batch].set(c_ctx)
    m = _ada_mod(cond, w_mod, b_mod).reshape(depth, 2 * SUBLANES, 3, d)
    mod = jnp.stack([jnp.broadcast_to(m[:, batch][:, None], (depth, batch, 3, d)), m[:, :batch]], axis=1)
    mod = jnp.swapaxes(mod, 2, 3)

    cos, sin = _rope_tables(n_lat, n_ctx, batch)
    perm = _rope_col_perm()
    qk_w = ATT_HEADS * 2 * ATT_QK_DIM
    seg = jnp.arange(w_in.shape[-1]) // BRANCH_W
    halved = (seg == 3) | (seg == 5) | (seg >= 7)
    w_in_p = jnp.concatenate([w_in[:, :, :qk_w][:, :, perm], w_in[:, :, qk_w:2 * qk_w][:, :, perm],
                              w_in[:, :, 2 * qk_w:]], axis=-1)
    w_in_p = (w_in_p * jnp.where(halved, 0.5, 1.0)).astype(BF16)

    n_dir = depth * 2 * S5_GROUPS
    ar, ai, cr, ci = _s5_discretise(s5_lam_re.reshape(n_dir, S5_STATE), s5_lam_im.reshape(n_dir, S5_STATE),
                                    s5_log_dt.reshape(n_dir, 1))
    gshape = (depth, 2, S5_GROUPS, S5_STATE)
    ar, ai, cr, ci = (a.reshape(gshape) for a in (ar, ai, cr, ci))
    bbr = cr[..., None] * s5_b_re - ci[..., None] * s5_b_im
    bbi = cr[..., None] * s5_b_im + ci[..., None] * s5_b_re
    nb, bg = S5_BLOCKS, S5_BLOCK_GROUPS
    eye_g = jnp.eye(bg, dtype=F32)

    def drive(bb):
        bb = bb.reshape(depth, 2, nb, bg, S5_STATE, S5_GROUP)
        return (bb.transpose(0, 1, 2, 3, 5, 4)[:, :, :, :, :, None, :]
                * eye_g[:, None, :, None]).reshape(depth, 2, nb, bg * S5_GROUP, S5_HALF)

    def read(cc):
        cc = cc.reshape(depth, nb, bg, S5_GROUP, S5_STATE)
        return (cc[:, :, :, :, None, :] * eye_g[:, None, :, None]).reshape(depth, nb, bg * S5_GROUP, S5_HALF)

    bd = jnp.concatenate([drive(bbr), drive(bbi)], axis=-1).astype(BF16)
    cd = jnp.concatenate([read(s5_c_re), -read(s5_c_im)], axis=-1).astype(BF16)
    ar = ar.reshape(depth, 2, nb, 1, S5_HALF)
    ai = ai.reshape(depth, 2, nb, 1, S5_HALF)

    w_gate = (0.5 * jnp.concatenate([_block_diag(lru_wa), _block_diag(lru_wx)], axis=-1)).astype(BF16)
    b_gate = 0.5 * jnp.concatenate([lru_ba, lru_bx], axis=-1)[:, :, None, :]
    w_glu_h = (0.5 * s5_w_glu).astype(BF16)
    b_glu_h = 0.5 * s5_b_glu
    w_branch_h = (0.5 * w_branch).astype(BF16)
    w_out_b = w_out.astype(BF16)

    for l in range(depth):
        lam_init = 0.8 - 0.6 * math.exp(-0.3 * l)
        q, k, v, za, xr, zr, us, zs, gl = _in_proj(xs, norm_g[l][None], mod[l], cos, sin, w_in_p[l],
                                                   n_ctx_chunks)
        ya = _attention(q, k, v, lam_qk[l], subln_g[l][None], lam_init, n_ctx)
        hl = _rglru(xr, conv_w[l], conv_b[l][None], w_gate[l], b_gate[l], lru_lam[l][:, None, :],
                    n_ctx_chunks)
        ys = _s5(us, bd[l], cd[l], ar[l], ai[l], n_ctx_chunks)
        xs = _merge(xs, mod[l], ya, hl, ys, us, za, zr, zs, gl, s5_d[l][None], w_glu_h[l],
                    b_glu_h[l][None], w_branch_h[l], w_out_b[l], n_ctx_chunks)

    out = _final_norm(xs, final_g[None], n_ctx_chunks)
    return jnp.swapaxes(out.reshape(n_lat, batch, d), 0, 1)
```

```python
import functools
import math

import jax
import jax.numpy as jnp
from jax import lax
from jax.experimental import pallas as pl
from jax.experimental.pallas import tpu as pltpu

F32 = jnp.float32
BF16 = jnp.bfloat16

EPS = 1e-6
GRID_W = 64
ATT_HEADS = 4
ATT_QK_DIM = 64
ATT_V_DIM = 2 * ATT_QK_DIM
BRANCH_W = 512
N_BRANCH = 3
ROPE_BASE = 10000.0
LRU_BLOCKS = 8
LRU_C = 8.0
CONV_W = 4
S5_GROUP = 16
S5_GROUPS = BRANCH_W // S5_GROUP
S5_STATE = 64
S5_LANES = S5_GROUPS * S5_STATE
S5_BLOCK_GROUPS = 8
S5_BLOCKS = S5_GROUPS // S5_BLOCK_GROUPS
S5_HALF = S5_BLOCK_GROUPS * S5_STATE

SUBLANES = 8
LANES = 128
CHUNK_T = 64
KEY_CHUNK = 256
VMEM_LIMIT = 56 * 1024 * 1024

LOG2E = math.log2(math.e)


def _cparams(sem):
    return pltpu.CompilerParams(dimension_semantics=sem, vmem_limit_bytes=VMEM_LIMIT)


def _sigmoid(z):
    return 1.0 / (1.0 + jnp.exp(-z))


def _silu(z):
    return z * _sigmoid(z)


def _mod_kernel(c_ref, w_ref, b_ref, o_ref):
    c = c_ref[...]
    o_ref[...] = jnp.dot(_silu(c), w_ref[...], preferred_element_type=F32,
                         precision=lax.Precision.HIGHEST) + b_ref[...]


def _ada_mod(cond, w_mod, b_mod):
    depth, d, d3 = w_mod.shape
    return pl.pallas_call(
        _mod_kernel,
        grid=(depth, d3 // d),
        in_specs=[pl.BlockSpec((2 * SUBLANES, d), lambda l, j: (0, 0)),
                  pl.BlockSpec((None, d, d), lambda l, j: (l, 0, j)),
                  pl.BlockSpec((None, 1, d), lambda l, j: (l, 0, j))],
        out_specs=pl.BlockSpec((None, 2 * SUBLANES, d), lambda l, j: (l, 0, j)),
        out_shape=jax.ShapeDtypeStruct((depth, 2 * SUBLANES, d3), F32),
        compiler_params=_cparams(("parallel", "parallel")),
        name="ada_mod",
    )(cond, w_mod, b_mod.reshape(depth, 1, d3))


def _inproj_kernel(x_ref, g_ref, mod_ref, cos_ref, sin_ref, w_ref,
                   q_ref, k_ref, v_ref, za_ref, xr_ref, zr_ref, us_ref, zs_ref, gl_ref,
                   h_scr, slab_scr, *, q_scale):
    x = x_ref[...]
    tm, d = x.shape
    nt = tm // SUBLANES
    y = x * lax.rsqrt(jnp.mean(x * x, axis=-1, keepdims=True) + EPS) * g_ref[...]
    y = y.reshape(nt, SUBLANES, d)
    h = y * (1.0 + mod_ref[1])[None] + mod_ref[0][None]
    h_scr[...] = h.reshape(tm, d).astype(BF16)

    def proj(j):
        return jnp.dot(h_scr[...], w_ref[:, j * BRANCH_W:(j + 1) * BRANCH_W],
                       preferred_element_type=F32)

    cos = cos_ref[...]
    sin = sin_ref[...]

    def to_batch_major(a, ref, slab0, rope, scale):
        for hh in range(ATT_HEADS):
            r = a[:, hh * LANES:(hh + 1) * LANES]
            if rope:
                r = (r * cos + pltpu.roll(r, LANES // 2, axis=1) * sin) * scale
            slab_scr[slab0 + hh] = r
            for b in range(SUBLANES):
                ref[b, :, hh * LANES:(hh + 1) * LANES] = (
                    slab_scr[slab0 + hh, pl.ds(b, nt, stride=SUBLANES), :].astype(BF16))

    to_batch_major(proj(0), q_ref, 0, True, q_scale)
    to_batch_major(proj(1), k_ref, ATT_HEADS, True, 1.0)
    to_batch_major(proj(2), v_ref, 2 * ATT_HEADS, False, 1.0)
    for j, ref in ((3, za_ref), (4, xr_ref), (5, zr_ref), (6, us_ref), (7, zs_ref)):
        ref[...] = proj(j).astype(BF16)
    for j in range(2 * N_BRANCH):
        gl_ref[:, j * BRANCH_W:(j + 1) * BRANCH_W] = proj(8 + j).astype(BF16)


def _in_proj(x, norm_g, mod, cos, sin, w_in, n_ctx_chunks):
    rows, d = x.shape
    tm = CHUNK_T * SUBLANES
    nch = rows // tm
    row_spec = lambda w: pl.BlockSpec((tm, w), lambda i: (i, 0))
    out_w = [BRANCH_W] * 5 + [N_BRANCH * d]
    bm_spec = pl.BlockSpec((SUBLANES, CHUNK_T, BRANCH_W), lambda i: (0, i, 0))
    bm_shape = jax.ShapeDtypeStruct((SUBLANES, rows // SUBLANES, BRANCH_W), BF16)
    return pl.pallas_call(
        functools.partial(_inproj_kernel, q_scale=ATT_QK_DIM ** -0.5 * LOG2E),
        grid=(nch,),
        in_specs=[row_spec(d),
                  pl.BlockSpec((1, d), lambda i: (0, 0)),
                  pl.BlockSpec((None, 3, SUBLANES, d),
                               lambda i: (jnp.where(i < n_ctx_chunks, 0, 1), 0, 0, 0)),
                  row_spec(LANES), row_spec(LANES),
                  pl.BlockSpec(w_in.shape, lambda i: (0, 0), pipeline_mode=pl.Buffered(1))],
        out_specs=[bm_spec] * 3 + [row_spec(w) for w in out_w],
        out_shape=[bm_shape] * 3 + [jax.ShapeDtypeStruct((rows, w), BF16) for w in out_w],
        scratch_shapes=[pltpu.VMEM((tm, d), BF16), pltpu.VMEM((3 * ATT_HEADS, tm, LANES), F32)],
        compiler_params=_cparams(("parallel",)),
        name="in_proj",
    )(x, norm_g, mod, cos, sin, w_in)


def _attn_kernel(lamqk_ref, g_ref, qc_ref, qn_ref, k_ref, v_ref, o_ref,
                 s0_scr, s1_scr, m0_scr, m1_scr, a0_scr, a1_scr, vext_scr, *,
                 tq, n_ctx_tiles, n_tiles, n_ctx, n_all, lam_init):
    i = pl.program_id(2)

    @pl.when(i == 0)
    def _():
        for off in range(0, n_all, LANES):
            vext_scr[:ATT_V_DIM, off:off + LANES] = v_ref[off:off + LANES, :].astype(F32).T.astype(BF16)
        vext_scr[ATT_V_DIM:, :] = jnp.ones((2 * SUBLANES, n_all), BF16)

    lq = lamqk_ref[...]
    lam = (jnp.exp(jnp.sum(lq[0:1] * lq[1:2], axis=-1, keepdims=True))
           - jnp.exp(jnp.sum(lq[2:3] * lq[3:4], axis=-1, keepdims=True)) + lam_init)

    def chunks(nk):
        return [(off, min(KEY_CHUNK, nk - off)) for off in range(0, nk, KEY_CHUNK)]

    cols = (slice(0, tq), slice(tq, 2 * tq))

    def split(q_ref):
        q = q_ref[...]
        lane = lax.broadcasted_iota(jnp.int32, q.shape, 1)
        comp0 = (lane % (LANES // 2)) < (LANES // 4)
        zero = jnp.zeros_like(q)
        return jnp.where(comp0, q, zero), jnp.where(comp0, zero, q)

    def score_chunk(qs, c, off, kc, s_dst, mm):
        st = lax.dot_general(k_ref[off:off + kc, :], qs[c], (((1,), (1,)), ((), ())),
                             preferred_element_type=F32)
        s_dst[off:off + kc, cols[c]] = st
        part = jnp.max(st.reshape(kc // SUBLANES, SUBLANES, tq), axis=0)
        return part if mm is None else jnp.maximum(mm, part)

    def score_done(c, m_dst, mm):
        m_dst[:, cols[c]] = jnp.broadcast_to(jnp.max(mm, axis=0, keepdims=True), (SUBLANES, tq))

    def weight_chunk(c, off, kc, s_src, m, acc):
        st = s_src[off:off + kc, cols[c]].reshape(kc // SUBLANES, SUBLANES, tq)
        pt = jnp.exp2(st - m[None]).reshape(kc, tq).astype(BF16)
        return acc + jnp.dot(vext_scr[:, off:off + kc], pt, preferred_element_type=F32)

    def epilogue(acc_src):
        ots = [acc_src[c, :ATT_V_DIM] / acc_src[c, ATT_V_DIM:ATT_V_DIM + 1] for c in range(2)]
        o = (ots[0] - lam * ots[1]).T
        o = o * lax.rsqrt(jnp.mean(o * o, axis=-1, keepdims=True) + EPS) * g_ref[...]
        o_ref[...] = (o * (1.0 - lam_init)).astype(BF16)

    acc0 = jnp.zeros((ATT_V_DIM + 2 * SUBLANES, tq), F32)

    def scores(q_ref, nk, s_dst, m_dst):
        qs = split(q_ref)
        for c in range(2):
            mm = None
            for off, kc in chunks(nk):
                mm = score_chunk(qs, c, off, kc, s_dst, mm)
            score_done(c, m_dst, mm)

    def weighted(nk, s_src, m_src, acc_dst):
        for c in range(2):
            acc = acc0
            for off, kc in chunks(nk):
                acc = weight_chunk(c, off, kc, s_src, m_src[:, cols[c]], acc)
            acc_dst[c] = acc

    def both(nk, s_src, m_src, acc_dst, q_ref, s_dst, m_dst):
        qs = split(q_ref)
        ms = [m_src[:, cols[c]] for c in range(2)]
        accs = [acc0, acc0]
        mms = [None, None]
        for off, kc in chunks(nk):
            for c in range(2):
                accs[c] = weight_chunk(c, off, kc, s_src, ms[c], accs[c])
                mms[c] = score_chunk(qs, c, off, kc, s_dst, mms[c])
        for c in range(2):
            acc_dst[c] = accs[c]
            score_done(c, m_dst, mms[c])

    slots = ((s0_scr, m0_scr, a0_scr), (s1_scr, m1_scr, a1_scr))
    for par in range(2):
        cur, nxt = slots[par], slots[1 - par]
        mine = i % 2 == par

        @pl.when(jnp.logical_and(mine, i == 0))
        def _(cur=cur):
            scores(qc_ref, n_ctx, *cur[:2])
            weighted(n_ctx, *cur)

        if n_ctx_tiles > 1:
            @pl.when(jnp.logical_and(mine, jnp.logical_and(i > 0, i < n_ctx_tiles)))
            def _(cur=cur, nxt=nxt):
                epilogue(nxt[2])
                scores(qc_ref, n_ctx, *cur[:2])
                weighted(n_ctx, *cur)

        @pl.when(jnp.logical_and(mine, i == n_ctx_tiles - 1))
        def _(nxt=nxt):
            scores(qn_ref, n_all, *nxt[:2])

        @pl.when(jnp.logical_and(mine, jnp.logical_and(i >= n_ctx_tiles, i < n_tiles)))
        def _(cur=cur, nxt=nxt):
            epilogue(nxt[2])
            both(n_all, *cur, qn_ref, *nxt[:2])

        @pl.when(jnp.logical_and(mine, i == n_tiles))
        def _(nxt=nxt):
            epilogue(nxt[2])


def _attention(q, k, v, lam_qk, subln_g, lam_init, n_ctx):
    batch, s_all, width = q.shape
    tq = min(256, n_ctx)
    nt = s_all // tq
    acc_shape = (2, ATT_V_DIM + 2 * SUBLANES, tq)
    return pl.pallas_call(
        functools.partial(_attn_kernel, tq=tq, n_ctx_tiles=n_ctx // tq, n_tiles=nt, n_ctx=n_ctx,
                          n_all=s_all, lam_init=lam_init),
        grid=(batch, ATT_HEADS, nt + 1),
        in_specs=[pl.BlockSpec((4, ATT_QK_DIM), lambda b, h, i: (0, 0)),
                  pl.BlockSpec((1, ATT_V_DIM), lambda b, h, i: (0, 0)),
                  pl.BlockSpec((None, tq, LANES), lambda b, h, i: (b, jnp.minimum(i, nt - 1), h)),
                  pl.BlockSpec((None, tq, LANES), lambda b, h, i: (b, jnp.minimum(i + 1, nt - 1), h)),
                  pl.BlockSpec((None, s_all, LANES), lambda b, h, i: (b, 0, h)),
                  pl.BlockSpec((None, s_all, LANES), lambda b, h, i: (b, 0, h))],
        out_specs=pl.BlockSpec((None, tq, LANES), lambda b, h, i: (b, jnp.maximum(i - 1, 0), h)),
        out_shape=jax.ShapeDtypeStruct((batch, s_all, width), BF16),
        scratch_shapes=[pltpu.VMEM((s_all, 2 * tq), F32), pltpu.VMEM((s_all, 2 * tq), F32),
                        pltpu.VMEM((SUBLANES, 2 * tq), F32), pltpu.VMEM((SUBLANES, 2 * tq), F32),
                        pltpu.VMEM(acc_shape, F32), pltpu.VMEM(acc_shape, F32),
                        pltpu.VMEM((ATT_V_DIM + 2 * SUBLANES, s_all), BF16)],
        compiler_params=_cparams(("parallel", "parallel", "arbitrary")),
        name="diff_attention",
    )(lam_qk, subln_g, q, q, k, v)


def _scan_chunk(d, j, n_ctx_chunks, nch):
    back = jnp.where(j < n_ctx_chunks, n_ctx_chunks - 1 - j, nch - 1 + n_ctx_chunks - j)
    return jnp.where(d == 0, j, back)


def _lru_kernel(xp_ref, xc_ref, xn_ref, cw_ref, cb_ref, wg_ref, bg_ref, lam_ref, o_ref,
                a_scr, b_scr, h_scr, *, n_ctx_chunks, nch):
    d = pl.program_id(0)
    j = pl.program_id(1)
    c = _scan_chunk(d, j, n_ctx_chunks, nch)
    tm = xc_ref.shape[0]
    first = jnp.logical_or(c == 0, c == n_ctx_chunks)
    last = jnp.logical_or(c == n_ctx_chunks - 1, c == nch - 1)
    xp = xp_ref[...].astype(F32) * jnp.where(first, 0.0, 1.0)
    xn = xn_ref[...].astype(F32)[:SUBLANES] * jnp.where(last, 0.0, 1.0)
    xe = jnp.concatenate([xp, xc_ref[...].astype(F32), xn], axis=0)
    u = cb_ref[...]
    for tap in range(CONV_W):
        u = u + cw_ref[tap:tap + 1, :] * xe[tap * SUBLANES:tap * SUBLANES + tm]
    gates = jnp.tanh(jnp.dot(u.astype(BF16), wg_ref[...], preferred_element_type=F32) + bg_ref[...])
    z = -lam_ref[...]
    softplus = jnp.maximum(z, 0.0) + jnp.log(1.0 + jnp.exp(-jnp.abs(z)))
    half_c = (-0.5 * LRU_C * LOG2E) * softplus
    a = jnp.exp2(half_c + half_c * gates[:, :BRANCH_W])
    half_u = 0.5 * u
    a_scr[...] = a
    b_scr[...] = jnp.exp2(0.5 * jnp.log2(1.0 - a * a)) * (half_u + half_u * gates[:, BRANCH_W:])

    @pl.when(j == 0)
    def _():
        h_scr[...] = jnp.zeros(h_scr.shape, F32)

    nt = tm // SUBLANES

    def step(t, h):
        te = jnp.where(d == 0, t, nt - 1 - t)
        r0 = pl.multiple_of(te * SUBLANES, SUBLANES)
        h = a_scr[pl.ds(r0, SUBLANES), :] * h + b_scr[pl.ds(r0, SUBLANES), :]
        o_ref[pl.ds(r0, SUBLANES), :] = h
        return h

    h_scr[...] = lax.fori_loop(0, nt, step, h_scr[...], unroll=8)


def _rglru(xr, conv_w, conv_b, w_gate, b_gate, lru_lam, n_ctx_chunks):
    rows, width = xr.shape
    tm = CHUNK_T * SUBLANES
    nch = rows // tm
    halo = 2 * SUBLANES
    per = tm // halo
    chunk = lambda d, j: _scan_chunk(d, j, n_ctx_chunks, nch)
    return pl.pallas_call(
        functools.partial(_lru_kernel, n_ctx_chunks=n_ctx_chunks, nch=nch),
        grid=(2, nch),
        in_specs=[pl.BlockSpec((halo, width), lambda d, j: (jnp.maximum(chunk(d, j) * per - 1, 0), 0)),
                  pl.BlockSpec((tm, width), lambda d, j: (chunk(d, j), 0)),
                  pl.BlockSpec((halo, width),
                               lambda d, j: (jnp.minimum((chunk(d, j) + 1) * per, nch * per - 1), 0)),
                  pl.BlockSpec((CONV_W, width), lambda d, j: (0, 0)),
                  pl.BlockSpec((1, width), lambda d, j: (0, 0)),
                  pl.BlockSpec((None, width, 2 * width), lambda d, j: (d, 0, 0)),
                  pl.BlockSpec((None, 1, 2 * width), lambda d, j: (d, 0, 0)),
                  pl.BlockSpec((None, 1, width), lambda d, j: (d, 0, 0))],
        out_specs=pl.BlockSpec((None, tm, width), lambda d, j: (d, chunk(d, j), 0)),
        out_shape=jax.ShapeDtypeStruct((2, rows, width), F32),
        scratch_shapes=[pltpu.VMEM((tm, width), F32), pltpu.VMEM((tm, width), F32),
                        pltpu.VMEM((SUBLANES, width), F32)],
        compiler_params=_cparams(("arbitrary", "arbitrary")),
        name="rglru",
    )(xr, xr, xr, conv_w, conv_b, w_gate, b_gate, lru_lam)


def _s5_kernel(us_ref, usn_ref, bd_ref, cd_ref, ar_ref, ai_ref, o_ref, drv0_scr, drv1_scr, h_scr):
    d = pl.program_id(0)
    j = pl.program_id(1)
    tm = us_ref.shape[0]
    nt = tm // SUBLANES
    bw = 2 * S5_HALF

    def drive(src_ref, dst, jj):
        dst[:, jj * bw:(jj + 1) * bw] = jnp.dot(src_ref[:, jj * LANES:(jj + 1) * LANES], bd_ref[jj],
                                                preferred_element_type=F32)

    @pl.when(j == 0)
    def _():
        h_scr[...] = jnp.zeros(h_scr.shape, F32)
        for jj in range(S5_BLOCKS):
            drive(us_ref, drv0_scr, jj)

    def scan(buf, jj, order):
        re = slice(jj * bw, jj * bw + S5_HALF)
        im = slice(jj * bw + S5_HALF, (jj + 1) * bw)
        ar = jnp.broadcast_to(ar_ref[jj], (SUBLANES, S5_HALF))
        ai = jnp.broadcast_to(ai_ref[jj], (SUBLANES, S5_HALF))
        hr = h_scr[:, re]
        hi = h_scr[:, im]
        for t in order:
            rws = slice(t * SUBLANES, (t + 1) * SUBLANES)
            hr, hi = (ar * hr - ai * hi + buf[rws, re], ar * hi + ai * hr + buf[rws, im])
            buf[rws, re] = hr
            buf[rws, im] = hi
        h_scr[:, re] = hr
        h_scr[:, im] = hi

    def readout(buf, jj):
        o_ref[:, jj * LANES:(jj + 1) * LANES] = jnp.dot(buf[:, jj * bw:(jj + 1) * bw].astype(BF16),
                                                        cd_ref[jj], preferred_element_type=F32)

    def run(cur, nxt, order):
        for stage in range(S5_BLOCKS + 1):
            if stage < S5_BLOCKS:
                scan(cur, stage, order)
                drive(usn_ref, nxt, stage)
            if stage >= 1:
                readout(cur, stage - 1)

    bufs = (drv0_scr, drv1_scr)
    for rev in range(2):
        order = range(nt - 1, -1, -1) if rev else range(nt)
        for par in range(2):
            @pl.when(jnp.logical_and(d == rev, j % 2 == par))
            def _(par=par, order=order):
                run(bufs[par], bufs[1 - par], order)


def _s5(us, bd, cd, ar, ai, n_ctx_chunks):
    rows, width = us.shape
    tm = CHUNK_T * SUBLANES
    nch = rows // tm
    chunk = lambda d, j: _scan_chunk(d, j, n_ctx_chunks, nch)
    return pl.pallas_call(
        _s5_kernel,
        grid=(2, nch),
        in_specs=[pl.BlockSpec((tm, width), lambda d, j: (chunk(d, j), 0)),
                  pl.BlockSpec((tm, width), lambda d, j: (chunk(d, jnp.minimum(j + 1, nch - 1)), 0)),
                  pl.BlockSpec((None,) + bd.shape[1:], lambda d, j: (d, 0, 0, 0)),
                  pl.BlockSpec(cd.shape, lambda d, j: (0, 0, 0)),
                  pl.BlockSpec((None,) + ar.shape[1:], lambda d, j: (d, 0, 0, 0)),
                  pl.BlockSpec((None,) + ai.shape[1:], lambda d, j: (d, 0, 0, 0))],
        out_specs=pl.BlockSpec((None, tm, width), lambda d, j: (d, chunk(d, j), 0)),
        out_shape=jax.ShapeDtypeStruct((2, rows, width), F32),
        scratch_shapes=[pltpu.VMEM((tm, 2 * S5_LANES), F32), pltpu.VMEM((tm, 2 * S5_LANES), F32),
                        pltpu.VMEM((SUBLANES, 2 * S5_LANES), F32)],
        compiler_params=_cparams(("arbitrary", "arbitrary")),
        name="s5_scan",
    )(us, us, bd, cd, ar, ai)


def _s5_disc_kernel(lr_ref, li_ref, ldt_ref, ar_ref, ai_ref, cr_ref, ci_ref):
    lr = lr_ref[...]
    li = li_ref[...]
    dt = jnp.exp(ldt_ref[...])
    mag = jnp.exp(lr * dt)
    ar = mag * jnp.cos(li * dt)
    ai = mag * jnp.sin(li * dt)
    den = lr * lr + li * li
    nr = ar - 1.0
    ar_ref[...] = ar
    ai_ref[...] = ai
    cr_ref[...] = (nr * lr + ai * li) / den
    ci_ref[...] = (ai * lr - nr * li) / den


def _s5_discretise(lam_re, lam_im, log_dt):
    n = lam_re.shape[0]
    shp = jax.ShapeDtypeStruct((n, S5_STATE), F32)
    return pl.pallas_call(_s5_disc_kernel, out_shape=[shp] * 4, name="s5_discretise")(
        lam_re, lam_im, jnp.broadcast_to(log_dt, (n, S5_STATE)))


def _gelu_tanh(y):
    return 0.5 * y * (1.0 + jnp.tanh(math.sqrt(2.0 / math.pi) * (y + 0.044715 * (y * y * y))))


def _merge_kernel(x_ref, mod_ref, ya_ref, hl_ref, ys_ref, us_ref, za_ref, zr_ref, zs_ref, gl_ref,
                  sd_ref, wglu_ref, bglu_ref, wbr_ref, wout_ref, fg_ref, o_ref, slab_scr, out_scr):
    tm, d = x_ref.shape
    y = ys_ref[0] + ys_ref[1] + sd_ref[...] * us_ref[...].astype(F32)
    g = _gelu_tanh(y)
    hg = 0.5 * g
    ys = hg + hg * jnp.tanh(jnp.dot(g.astype(BF16), wglu_ref[...], preferred_element_type=F32)
                            + bglu_ref[...])
    yr = hl_ref[0] + hl_ref[1]
    for hh in range(ATT_HEADS):
        for b in range(SUBLANES):
            slab_scr[hh, pl.ds(b, tm // SUBLANES, stride=SUBLANES), :] = (
                ya_ref[b, :, hh * LANES:(hh + 1) * LANES].astype(F32))
    ya = jnp.concatenate([slab_scr[hh] for hh in range(ATT_HEADS)], axis=1)
    acc = jnp.zeros((tm, d), F32)
    for n, (yn, zn_ref) in enumerate(((ya, za_ref), (yr, zr_ref), (ys, zs_ref))):
        zh = zn_ref[...].astype(F32)
        yz = yn * zh
        hyb = jnp.dot((yz + yz * jnp.tanh(zh)).astype(BF16), wbr_ref[n], preferred_element_type=F32)
        acc = acc + (hyb + hyb * jnp.tanh(gl_ref[:, n * d:(n + 1) * d].astype(F32)))
    upd = jnp.dot(acc.astype(BF16), wout_ref[...], preferred_element_type=F32)
    split = lambda a: a.reshape(tm // SUBLANES, SUBLANES, d)
    xn = (split(x_ref[...]) + mod_ref[2][None] * split(upd)).reshape(tm, d)
    if fg_ref is None:
        o_ref[...] = xn
    else:
        y = xn * lax.rsqrt(jnp.mean(xn * xn, axis=-1, keepdims=True) + EPS) * fg_ref[...]
        for s in range(d // LANES):
            out_scr[s] = y[:, s * LANES:(s + 1) * LANES]
            for b in range(SUBLANES):
                o_ref[b, :, s * LANES:(s + 1) * LANES] = out_scr[s, pl.ds(b, tm // SUBLANES, stride=SUBLANES), :]


def _merge_mid_kernel(*refs):
    _merge_kernel(*refs[:15], None, refs[15], refs[16], None)


def _merge(x, mod, ya, hl, ys, us, za, zr, zs, gl, s5_d, w_glu, b_glu, w_branch, w_out, n_ctx_chunks,
           final_g=None):
    rows, d = x.shape
    tm = CHUNK_T * SUBLANES
    nch = rows // tm
    w = BRANCH_W
    last = final_g is not None
    first = n_ctx_chunks if last else 0
    row_spec = lambda n: pl.BlockSpec((tm, n), lambda i: (i + first, 0))
    pair_spec = pl.BlockSpec((2, tm, w), lambda i: (0, i + first, 0))
    full = lambda a: pl.BlockSpec(a.shape, lambda i: (0,) * a.ndim)
    in_specs = [row_spec(d),
                pl.BlockSpec((None, 3, SUBLANES, d),
                             lambda i: (jnp.where(i + first < n_ctx_chunks, 0, 1), 0, 0, 0)),
                pl.BlockSpec((SUBLANES, CHUNK_T, w), lambda i: (0, i + first, 0)),
                pair_spec, pair_spec, row_spec(w), row_spec(w), row_spec(w),
                row_spec(w), row_spec(N_BRANCH * d),
                full(s5_d), full(w_glu), full(b_glu), full(w_branch), full(w_out)]
    args = [x, mod, ya, hl, ys, us, za, zr, zs, gl, s5_d, w_glu, b_glu, w_branch, w_out]
    scratch = [pltpu.VMEM((ATT_HEADS, tm, LANES), F32)]
    if last:
        return pl.pallas_call(
            _merge_kernel,
            grid=(nch - first,),
            in_specs=in_specs + [full(final_g)],
            out_specs=pl.BlockSpec((SUBLANES, CHUNK_T, d), lambda i: (0, i, 0)),
            out_shape=jax.ShapeDtypeStruct((SUBLANES, (nch - first) * CHUNK_T, d), F32),
            scratch_shapes=scratch + [pltpu.VMEM((d // LANES, tm, LANES), F32)],
            compiler_params=_cparams(("parallel",)),
            name="gated_merge_final",
        )(*args, final_g)
    return pl.pallas_call(
        _merge_mid_kernel,
        grid=(nch,),
        in_specs=in_specs,
        out_specs=row_spec(d),
        out_shape=jax.ShapeDtypeStruct((rows, d), F32),
        scratch_shapes=scratch,
        compiler_params=_cparams(("parallel",)),
        name="gated_merge",
    )(*args)


def _rope_tables(n_lat, n_ctx, batch):
    n_freq = ATT_QK_DIM // 4
    tl = jnp.arange(n_lat)
    inv = ROPE_BASE ** (-jnp.arange(n_freq, dtype=F32) / n_freq)
    ang = jnp.concatenate([(tl // GRID_W).astype(F32)[:, None] * inv,
                           (tl % GRID_W).astype(F32)[:, None] * inv], axis=-1)
    cos = jnp.tile(jnp.cos(ang), (1, 4))
    sin = jnp.tile(jnp.sin(ang), (1, 4)) * jnp.where(jnp.arange(LANES) < LANES // 2, -1.0, 1.0)
    cos = jnp.concatenate([jnp.ones((n_ctx, LANES), F32), cos], axis=0)
    sin = jnp.concatenate([jnp.zeros((n_ctx, LANES), F32), sin], axis=0)
    rep = lambda a: jnp.broadcast_to(a[:, None, :], (a.shape[0], batch, LANES)).reshape(-1, LANES)
    return rep(cos), rep(sin)


def _rope_col_perm():
    j = jnp.arange(LANES)
    within = ((j // 32) % 2) * ATT_QK_DIM + 2 * (j % 32) + j // 64
    return (jnp.arange(ATT_HEADS)[:, None] * LANES + within[None, :]).reshape(-1)


def _block_diag(w):
    *lead, n, c, _ = w.shape
    dense = w[..., :, :, None, :] * jnp.eye(n, dtype=w.dtype)[:, None, :, None]
    return dense.reshape(*lead, n * c, n * c)


def kernel(x, c, ctx, c_ctx, w_mod, b_mod, norm_g, w_in, lam_qk, subln_g, conv_w, conv_b, lru_wa, lru_ba,
           lru_wx, lru_bx, lru_lam, s5_lam_re, s5_lam_im, s5_log_dt, s5_b_re, s5_b_im, s5_c_re, s5_c_im,
           s5_d, s5_w_glu, s5_b_glu, w_branch, w_out, final_g):
    batch, n_lat, d = x.shape
    n_ctx = ctx.shape[1]
    depth = w_mod.shape[0]
    assert batch == SUBLANES and n_lat % CHUNK_T == 0 and n_ctx % CHUNK_T == 0
    n_ctx_chunks = n_ctx // CHUNK_T
    s_all = n_ctx + n_lat
    rows = s_all * batch

    xs = jnp.concatenate([jnp.swapaxes(ctx, 0, 1), jnp.swapaxes(x, 0, 1)], axis=0).reshape(rows, d)

    cond = jnp.zeros((2 * SUBLANES, d), F32).at[:batch].set(c).at[batch].set(c_ctx)
    m = _ada_mod(cond, w_mod, b_mod).reshape(depth, 2 * SUBLANES, 3, d)
    mod = jnp.stack([jnp.broadcast_to(m[:, batch][:, None], (depth, batch, 3, d)), m[:, :batch]], axis=1)
    mod = jnp.swapaxes(mod, 2, 3)

    cos, sin = _rope_tables(n_lat, n_ctx, batch)
    perm = _rope_col_perm()
    qk_w = ATT_HEADS * 2 * ATT_QK_DIM
    seg = jnp.arange(w_in.shape[-1]) // BRANCH_W
    halved = (seg == 3) | (seg == 5) | (seg >= 7)
    w_in_p = jnp.concatenate([w_in[:, :, :qk_w][:, :, perm], w_in[:, :, qk_w:2 * qk_w][:, :, perm],
                              w_in[:, :, 2 * qk_w:]], axis=-1)
    w_in_p = (w_in_p * jnp.where(halved, 0.5, 1.0)).astype(BF16)

    n_dir = depth * 2 * S5_GROUPS
    ar, ai, cr, ci = _s5_discretise(s5_lam_re.reshape(n_dir, S5_STATE), s5_lam_im.reshape(n_dir, S5_STATE),
                                    s5_log_dt.reshape(n_dir, 1))
    gshape = (depth, 2, S5_GROUPS, S5_STATE)
    ar, ai, cr, ci = (a.reshape(gshape) for a in (ar, ai, cr, ci))
    bbr = cr[..., None] * s5_b_re - ci[..., None] * s5_b_im
    bbi = cr[..., None] * s5_b_im + ci[..., None] * s5_b_re
    nb, bg = S5_BLOCKS, S5_BLOCK_GROUPS
    eye_g = jnp.eye(bg, dtype=F32)

    def drive(bb):
        bb = bb.reshape(depth, 2, nb, bg, S5_STATE, S5_GROUP)
        return (bb.transpose(0, 1, 2, 3, 5, 4)[:, :, :, :, :, None, :]
                * eye_g[:, None, :, None]).reshape(depth, 2, nb, bg * S5_GROUP, S5_HALF)

    def read(cc):
        cc = cc.reshape(depth, nb, bg, S5_GROUP, S5_STATE)
        return (cc.transpose(0, 1, 2, 4, 3)[:, :, :, :, None, :]
                * eye_g[:, None, :, None]).reshape(depth, nb, S5_HALF, bg * S5_GROUP)

    bd = jnp.concatenate([drive(bbr), drive(bbi)], axis=-1).astype(BF16)
    cd = jnp.concatenate([read(s5_c_re), -read(s5_c_im)], axis=2).astype(BF16)
    ar = ar.reshape(depth, 2, nb, 1, S5_HALF)
    ai = ai.reshape(depth, 2, nb, 1, S5_HALF)

    w_gate = (0.5 * jnp.concatenate([_block_diag(lru_wa), _block_diag(lru_wx)], axis=-1)).astype(BF16)
    b_gate = 0.5 * jnp.concatenate([lru_ba, lru_bx], axis=-1)[:, :, None, :]
    w_glu_h = (0.5 * s5_w_glu).astype(BF16)
    b_glu_h = 0.5 * s5_b_glu
    w_branch_h = (0.5 * w_branch).astype(BF16)
    w_out_b = w_out.astype(BF16)

    for l in range(depth):
        lam_init = 0.8 - 0.6 * math.exp(-0.3 * l)
        q, k, v, za, xr, zr, us, zs, gl = _in_proj(xs, norm_g[l][None], mod[l], cos, sin, w_in_p[l],
                                                   n_ctx_chunks)
        ya = _attention(q, k, v, lam_qk[l], subln_g[l][None], lam_init, n_ctx)
        hl = _rglru(xr, conv_w[l], conv_b[l][None], w_gate[l], b_gate[l], lru_lam[l][:, None, :],
                    n_ctx_chunks)
        ys = _s5(us, bd[l], cd[l], ar[l], ai[l], n_ctx_chunks)
        xs = _merge(xs, mod[l], ya, hl, ys, us, za, zr, zs, gl, s5_d[l][None], w_glu_h[l],
                    b_glu_h[l][None], w_branch_h[l], w_out_b[l], n_ctx_chunks,
                    final_g=final_g[None] if l == depth - 1 else None)
    return xs
```

```python
import functools
import math

import jax
import jax.numpy as jnp
from jax import lax
from jax.experimental import pallas as pl
from jax.experimental.pallas import tpu as pltpu

F32 = jnp.float32
BF16 = jnp.bfloat16

EPS = 1e-6
GRID_W = 64
ATT_HEADS = 4
ATT_QK_DIM = 64
ATT_V_DIM = 2 * ATT_QK_DIM
BRANCH_W = 512
N_BRANCH = 3
ROPE_BASE = 10000.0
LRU_BLOCKS = 8
LRU_C = 8.0
CONV_W = 4
S5_GROUP = 16
S5_GROUPS = BRANCH_W // S5_GROUP
S5_STATE = 64
S5_LANES = S5_GROUPS * S5_STATE
S5_BLOCK_GROUPS = 8
S5_BLOCKS = S5_GROUPS // S5_BLOCK_GROUPS
S5_HALF = S5_BLOCK_GROUPS * S5_STATE

SUBLANES = 8
LANES = 128
CHUNK_T = 64
KEY_CHUNK = 256
VMEM_LIMIT = 56 * 1024 * 1024

LOG2E = math.log2(math.e)


def _cparams(sem):
    return pltpu.CompilerParams(dimension_semantics=sem, vmem_limit_bytes=VMEM_LIMIT)


def _sigmoid(z):
    return 1.0 / (1.0 + jnp.exp(-z))


def _silu(z):
    return z * _sigmoid(z)


def _mod_kernel(c_ref, w_ref, b_ref, o_ref):
    c = c_ref[...]
    o_ref[...] = jnp.dot(_silu(c), w_ref[...], preferred_element_type=F32,
                         precision=lax.Precision.HIGHEST) + b_ref[...]


def _ada_mod(cond, w_mod, b_mod):
    depth, d, d3 = w_mod.shape
    return pl.pallas_call(
        _mod_kernel,
        grid=(depth, d3 // d),
        in_specs=[pl.BlockSpec((2 * SUBLANES, d), lambda l, j: (0, 0)),
                  pl.BlockSpec((None, d, d), lambda l, j: (l, 0, j)),
                  pl.BlockSpec((None, 1, d), lambda l, j: (l, 0, j))],
        out_specs=pl.BlockSpec((None, 2 * SUBLANES, d), lambda l, j: (l, 0, j)),
        out_shape=jax.ShapeDtypeStruct((depth, 2 * SUBLANES, d3), F32),
        compiler_params=_cparams(("parallel", "parallel")),
        name="ada_mod",
    )(cond, w_mod, b_mod.reshape(depth, 1, d3))


def _inproj_kernel(x_ref, g_ref, mod_ref, cos_ref, sin_ref, w_ref,
                   q_ref, k_ref, v_ref, za_ref, xr_ref, zr_ref, us_ref, zs_ref, gl_ref,
                   h_scr, slab_scr, *, q_scale):
    x = x_ref[...]
    tm, d = x.shape
    nt = tm // SUBLANES
    y = x * lax.rsqrt(jnp.mean(x * x, axis=-1, keepdims=True) + EPS) * g_ref[...]
    y = y.reshape(nt, SUBLANES, d)
    h = y * (1.0 + mod_ref[1])[None] + mod_ref[0][None]
    h_scr[...] = h.reshape(tm, d).astype(BF16)

    def proj(j):
        return jnp.dot(h_scr[...], w_ref[:, j * BRANCH_W:(j + 1) * BRANCH_W],
                       preferred_element_type=F32)

    cos = cos_ref[...]
    sin = sin_ref[...]
    even = lax.broadcasted_iota(jnp.int32, cos.shape, 1) % 2 == 0

    def to_batch_major(a, ref, slab0, rope, scale):
        for hh in range(ATT_HEADS):
            r = a[:, hh * LANES:(hh + 1) * LANES]
            if rope:
                partner = jnp.where(even, pltpu.roll(r, LANES - 1, axis=1), pltpu.roll(r, 1, axis=1))
                r = (r * cos + partner * sin) * scale
            slab_scr[slab0 + hh] = r
            for b in range(SUBLANES):
                ref[b, :, hh * LANES:(hh + 1) * LANES] = (
                    slab_scr[slab0 + hh, pl.ds(b, nt, stride=SUBLANES), :].astype(BF16))

    to_batch_major(proj(0), q_ref, 0, True, q_scale)
    to_batch_major(proj(1), k_ref, ATT_HEADS, True, 1.0)
    to_batch_major(proj(2), v_ref, 2 * ATT_HEADS, False, 1.0)
    for j, ref in ((3, za_ref), (4, xr_ref), (5, zr_ref), (6, us_ref), (7, zs_ref)):
        ref[...] = proj(j).astype(BF16)
    for j in range(2 * N_BRANCH):
        gl_ref[:, j * BRANCH_W:(j + 1) * BRANCH_W] = proj(8 + j).astype(BF16)


def _in_proj(x, norm_g, mod, cos, sin, w_in, n_ctx_chunks):
    rows, d = x.shape
    tm = CHUNK_T * SUBLANES
    nch = rows // tm
    row_spec = lambda w: pl.BlockSpec((tm, w), lambda i: (i, 0))
    out_w = [BRANCH_W] * 5 + [N_BRANCH * d]
    bm_spec = pl.BlockSpec((SUBLANES, CHUNK_T, BRANCH_W), lambda i: (0, i, 0))
    bm_shape = jax.ShapeDtypeStruct((SUBLANES, rows // SUBLANES, BRANCH_W), BF16)
    return pl.pallas_call(
        functools.partial(_inproj_kernel, q_scale=ATT_QK_DIM ** -0.5 * LOG2E),
        grid=(nch,),
        in_specs=[row_spec(d),
                  pl.BlockSpec((1, d), lambda i: (0, 0)),
                  pl.BlockSpec((None, 3, SUBLANES, d),
                               lambda i: (jnp.where(i < n_ctx_chunks, 0, 1), 0, 0, 0)),
                  row_spec(LANES), row_spec(LANES),
                  pl.BlockSpec(w_in.shape, lambda i: (0, 0), pipeline_mode=pl.Buffered(1))],
        out_specs=[bm_spec] * 3 + [row_spec(w) for w in out_w],
        out_shape=[bm_shape] * 3 + [jax.ShapeDtypeStruct((rows, w), BF16) for w in out_w],
        scratch_shapes=[pltpu.VMEM((tm, d), BF16), pltpu.VMEM((3 * ATT_HEADS, tm, LANES), F32)],
        compiler_params=_cparams(("parallel",)),
        name="in_proj",
    )(x, norm_g, mod, cos, sin, w_in)


def _attn_kernel(lamqk_ref, g_ref, qc_ref, qn_ref, k_ref, v_ref, o_ref,
                 s0_scr, s1_scr, m0_scr, m1_scr, a0_scr, a1_scr, vext_scr, *,
                 tq, n_ctx_tiles, n_tiles, n_ctx, n_all, lam_init):
    i = pl.program_id(2)

    @pl.when(i == 0)
    def _():
        for off in range(0, n_all, LANES):
            vext_scr[:ATT_V_DIM, off:off + LANES] = v_ref[off:off + LANES, :].astype(F32).T.astype(BF16)
        vext_scr[ATT_V_DIM:, :] = jnp.ones((2 * SUBLANES, n_all), BF16)

    lq = lamqk_ref[...]
    lam = (jnp.exp(jnp.sum(lq[0:1] * lq[1:2], axis=-1, keepdims=True))
           - jnp.exp(jnp.sum(lq[2:3] * lq[3:4], axis=-1, keepdims=True)) + lam_init)

    def chunks(nk):
        return [(off, min(KEY_CHUNK, nk - off)) for off in range(0, nk, KEY_CHUNK)]

    cols = (slice(0, tq), slice(tq, 2 * tq))

    def split(q_ref):
        qt = q_ref[...].astype(F32).T
        row = lax.broadcasted_iota(jnp.int32, qt.shape, 0)
        comp0 = row < ATT_QK_DIM
        return (jnp.where(comp0, qt, 0.0).astype(BF16), jnp.where(comp0, 0.0, qt).astype(BF16))

    def score_chunk(qs, c, off, kc, s_dst, mm):
        st = jnp.dot(k_ref[off:off + kc, :], qs[c], preferred_element_type=F32)
        s_dst[off:off + kc, cols[c]] = st
        part = jnp.max(st.reshape(kc // SUBLANES, SUBLANES, tq), axis=0)
        return part if mm is None else jnp.maximum(mm, part)

    def score_done(c, m_dst, mm):
        m_dst[:, cols[c]] = jnp.broadcast_to(jnp.max(mm, axis=0, keepdims=True), (SUBLANES, tq))

    def weight_chunk(c, off, kc, s_src, m, acc):
        st = s_src[off:off + kc, cols[c]].reshape(kc // SUBLANES, SUBLANES, tq)
        pt = jnp.exp2(st - m[None]).reshape(kc, tq).astype(BF16)
        return acc + jnp.dot(vext_scr[:, off:off + kc], pt, preferred_element_type=F32)

    def epilogue(acc_src):
        ots = [acc_src[c, :ATT_V_DIM] / acc_src[c, ATT_V_DIM:ATT_V_DIM + 1] for c in range(2)]
        o = (ots[0] - lam * ots[1]).T
        o = o * lax.rsqrt(jnp.mean(o * o, axis=-1, keepdims=True) + EPS) * g_ref[...]
        o_ref[...] = (o * (1.0 - lam_init)).astype(BF16)

    acc0 = jnp.zeros((ATT_V_DIM + 2 * SUBLANES, tq), F32)

    def scores(q_ref, nk, s_dst, m_dst):
        qs = split(q_ref)
        for c in range(2):
            mm = None
            for off, kc in chunks(nk):
                mm = score_chunk(qs, c, off, kc, s_dst, mm)
            score_done(c, m_dst, mm)

    def weighted(nk, s_src, m_src, acc_dst):
        for c in range(2):
            acc = acc0
            for off, kc in chunks(nk):
                acc = weight_chunk(c, off, kc, s_src, m_src[:, cols[c]], acc)
            acc_dst[c] = acc

    def both(nk, s_src, m_src, acc_dst, q_ref, s_dst, m_dst):
        qs = split(q_ref)
        ms = [m_src[:, cols[c]] for c in range(2)]
        accs = [acc0, acc0]
        mms = [None, None]
        for off, kc in chunks(nk):
            for c in range(2):
                accs[c] = weight_chunk(c, off, kc, s_src, ms[c], accs[c])
                mms[c] = score_chunk(qs, c, off, kc, s_dst, mms[c])
        for c in range(2):
            acc_dst[c] = accs[c]
            score_done(c, m_dst, mms[c])

    slots = ((s0_scr, m0_scr, a0_scr), (s1_scr, m1_scr, a1_scr))
    for par in range(2):
        cur, nxt = slots[par], slots[1 - par]
        mine = i % 2 == par

        @pl.when(jnp.logical_and(mine, i == 0))
        def _(cur=cur):
            scores(qc_ref, n_ctx, *cur[:2])
            weighted(n_ctx, *cur)

        if n_ctx_tiles > 1:
            @pl.when(jnp.logical_and(mine, jnp.logical_and(i > 0, i < n_ctx_tiles)))
            def _(cur=cur, nxt=nxt):
                epilogue(nxt[2])
                scores(qc_ref, n_ctx, *cur[:2])
                weighted(n_ctx, *cur)

        @pl.when(jnp.logical_and(mine, i == n_ctx_tiles - 1))
        def _(nxt=nxt):
            scores(qn_ref, n_all, *nxt[:2])

        @pl.when(jnp.logical_and(mine, jnp.logical_and(i >= n_ctx_tiles, i < n_tiles)))
        def _(cur=cur, nxt=nxt):
            epilogue(nxt[2])
            both(n_all, *cur, qn_ref, *nxt[:2])

        @pl.when(jnp.logical_and(mine, i == n_tiles))
        def _(nxt=nxt):
            epilogue(nxt[2])


def _attention(q, k, v, lam_qk, subln_g, lam_init, n_ctx):
    batch, s_all, width = q.shape
    tq = min(256, n_ctx)
    nt = s_all // tq
    acc_shape = (2, ATT_V_DIM + 2 * SUBLANES, tq)
    return pl.pallas_call(
        functools.partial(_attn_kernel, tq=tq, n_ctx_tiles=n_ctx // tq, n_tiles=nt, n_ctx=n_ctx,
                          n_all=s_all, lam_init=lam_init),
        grid=(batch, ATT_HEADS, nt + 1),
        in_specs=[pl.BlockSpec((4, ATT_QK_DIM), lambda b, h, i: (0, 0)),
                  pl.BlockSpec((1, ATT_V_DIM), lambda b, h, i: (0, 0)),
                  pl.BlockSpec((None, tq, LANES), lambda b, h, i: (b, jnp.minimum(i, nt - 1), h)),
                  pl.BlockSpec((None, tq, LANES), lambda b, h, i: (b, jnp.minimum(i + 1, nt - 1), h)),
                  pl.BlockSpec((None, s_all, LANES), lambda b, h, i: (b, 0, h)),
                  pl.BlockSpec((None, s_all, LANES), lambda b, h, i: (b, 0, h))],
        out_specs=pl.BlockSpec((None, tq, LANES), lambda b, h, i: (b, jnp.maximum(i - 1, 0), h)),
        out_shape=jax.ShapeDtypeStruct((batch, s_all, width), BF16),
        scratch_shapes=[pltpu.VMEM((s_all, 2 * tq), F32), pltpu.VMEM((s_all, 2 * tq), F32),
                        pltpu.VMEM((SUBLANES, 2 * tq), F32), pltpu.VMEM((SUBLANES, 2 * tq), F32),
                        pltpu.VMEM(acc_shape, F32), pltpu.VMEM(acc_shape, F32),
                        pltpu.VMEM((ATT_V_DIM + 2 * SUBLANES, s_all), BF16)],
        compiler_params=_cparams(("parallel", "parallel", "arbitrary")),
        name="diff_attention",
    )(lam_qk, subln_g, q, q, k, v)


def _scan_chunk(d, j, n_ctx_chunks, nch):
    back = jnp.where(j < n_ctx_chunks, n_ctx_chunks - 1 - j, nch - 1 + n_ctx_chunks - j)
    return jnp.where(d == 0, j, back)


def _lru_kernel(xp_ref, xc_ref, xn_ref, cw_ref, cb_ref, wg_ref, bg_ref, lam_ref, o_ref,
                a_scr, b_scr, h_scr, *, n_ctx_chunks, nch):
    d = pl.program_id(0)
    j = pl.program_id(1)
    c = _scan_chunk(d, j, n_ctx_chunks, nch)
    tm = xc_ref.shape[0]
    first = jnp.logical_or(c == 0, c == n_ctx_chunks)
    last = jnp.logical_or(c == n_ctx_chunks - 1, c == nch - 1)
    xp = xp_ref[...].astype(F32) * jnp.where(first, 0.0, 1.0)
    xn = xn_ref[...].astype(F32)[:SUBLANES] * jnp.where(last, 0.0, 1.0)
    xe = jnp.concatenate([xp, xc_ref[...].astype(F32), xn], axis=0)
    u = cb_ref[...]
    for tap in range(CONV_W):
        u = u + cw_ref[tap:tap + 1, :] * xe[tap * SUBLANES:tap * SUBLANES + tm]
    gates = jnp.tanh(jnp.dot(u.astype(BF16), wg_ref[...], preferred_element_type=F32) + bg_ref[...])
    z = -lam_ref[...]
    softplus = jnp.maximum(z, 0.0) + jnp.log(1.0 + jnp.exp(-jnp.abs(z)))
    half_c = (-0.5 * LRU_C * LOG2E) * softplus
    a = jnp.exp2(half_c + half_c * gates[:, :BRANCH_W])
    half_u = 0.5 * u
    a_scr[...] = a
    b_scr[...] = jnp.exp2(0.5 * jnp.log2(1.0 - a * a)) * (half_u + half_u * gates[:, BRANCH_W:])

    @pl.when(j == 0)
    def _():
        h_scr[...] = jnp.zeros(h_scr.shape, F32)

    nt = tm // SUBLANES

    def step(t, h):
        te = jnp.where(d == 0, t, nt - 1 - t)
        r0 = pl.multiple_of(te * SUBLANES, SUBLANES)
        h = a_scr[pl.ds(r0, SUBLANES), :] * h + b_scr[pl.ds(r0, SUBLANES), :]
        o_ref[pl.ds(r0, SUBLANES), :] = h
        return h

    h_scr[...] = lax.fori_loop(0, nt, step, h_scr[...], unroll=8)


def _rglru(xr, conv_w, conv_b, w_gate, b_gate, lru_lam, n_ctx_chunks):
    rows, width = xr.shape
    tm = CHUNK_T * SUBLANES
    nch = rows // tm
    halo = 2 * SUBLANES
    per = tm // halo
    chunk = lambda d, j: _scan_chunk(d, j, n_ctx_chunks, nch)
    return pl.pallas_call(
        functools.partial(_lru_kernel, n_ctx_chunks=n_ctx_chunks, nch=nch),
        grid=(2, nch),
        in_specs=[pl.BlockSpec((halo, width), lambda d, j: (jnp.maximum(chunk(d, j) * per - 1, 0), 0)),
                  pl.BlockSpec((tm, width), lambda d, j: (chunk(d, j), 0)),
                  pl.BlockSpec((halo, width),
                               lambda d, j: (jnp.minimum((chunk(d, j) + 1) * per, nch * per - 1), 0)),
                  pl.BlockSpec((CONV_W, width), lambda d, j: (0, 0)),
                  pl.BlockSpec((1, width), lambda d, j: (0, 0)),
                  pl.BlockSpec((None, width, 2 * width), lambda d, j: (d, 0, 0)),
                  pl.BlockSpec((None, 1, 2 * width), lambda d, j: (d, 0, 0)),
                  pl.BlockSpec((None, 1, width), lambda d, j: (d, 0, 0))],
        out_specs=pl.BlockSpec((None, tm, width), lambda d, j: (d, chunk(d, j), 0)),
        out_shape=jax.ShapeDtypeStruct((2, rows, width), F32),
        scratch_shapes=[pltpu.VMEM((tm, width), F32), pltpu.VMEM((tm, width), F32),
                        pltpu.VMEM((SUBLANES, width), F32)],
        compiler_params=_cparams(("arbitrary", "arbitrary")),
        name="rglru",
    )(xr, xr, xr, conv_w, conv_b, w_gate, b_gate, lru_lam)


def _s5_kernel(us_ref, usn_ref, bd_ref, cd_ref, ar_ref, ai_ref, o_ref, drv0_scr, drv1_scr, h_scr):
    d = pl.program_id(0)
    j = pl.program_id(1)
    tm = us_ref.shape[0]
    nt = tm // SUBLANES
    bw = 2 * S5_HALF

    def drive(src_ref, dst, jj):
        dst[:, jj * bw:(jj + 1) * bw] = jnp.dot(src_ref[:, jj * LANES:(jj + 1) * LANES], bd_ref[jj],
                                                preferred_element_type=F32)

    @pl.when(j == 0)
    def _():
        h_scr[...] = jnp.zeros(h_scr.shape, F32)
        for jj in range(S5_BLOCKS):
            drive(us_ref, drv0_scr, jj)

    def scan(buf, jj, order):
        re = slice(jj * bw, jj * bw + S5_HALF)
        im = slice(jj * bw + S5_HALF, (jj + 1) * bw)
        ar = jnp.broadcast_to(ar_ref[jj], (SUBLANES, S5_HALF))
        ai = jnp.broadcast_to(ai_ref[jj], (SUBLANES, S5_HALF))
        hr = h_scr[:, re]
        hi = h_scr[:, im]
        for t in order:
            rws = slice(t * SUBLANES, (t + 1) * SUBLANES)
            hr, hi = (ar * hr - ai * hi + buf[rws, re], ar * hi + ai * hr + buf[rws, im])
            buf[rws, re] = hr
            buf[rws, im] = hi
        h_scr[:, re] = hr
        h_scr[:, im] = hi

    def readout(buf, jj):
        o_ref[:, jj * LANES:(jj + 1) * LANES] = jnp.dot(buf[:, jj * bw:(jj + 1) * bw].astype(BF16),
                                                        cd_ref[jj], preferred_element_type=F32)

    def run(cur, nxt, order):
        for stage in range(S5_BLOCKS + 1):
            if stage < S5_BLOCKS:
                scan(cur, stage, order)
                drive(usn_ref, nxt, stage)
            if stage >= 1:
                readout(cur, stage - 1)

    bufs = (drv0_scr, drv1_scr)
    for rev in range(2):
        order = range(nt - 1, -1, -1) if rev else range(nt)
        for par in range(2):
            @pl.when(jnp.logical_and(d == rev, j % 2 == par))
            def _(par=par, order=order):
                run(bufs[par], bufs[1 - par], order)


def _s5(us, bd, cd, ar, ai, n_ctx_chunks):
    rows, width = us.shape
    tm = CHUNK_T * SUBLANES
    nch = rows // tm
    chunk = lambda d, j: _scan_chunk(d, j, n_ctx_chunks, nch)
    return pl.pallas_call(
        _s5_kernel,
        grid=(2, nch),
        in_specs=[pl.BlockSpec((tm, width), lambda d, j: (chunk(d, j), 0)),
                  pl.BlockSpec((tm, width), lambda d, j: (chunk(d, jnp.minimum(j + 1, nch - 1)), 0)),
                  pl.BlockSpec((None,) + bd.shape[1:], lambda d, j: (d, 0, 0, 0)),
                  pl.BlockSpec(cd.shape, lambda d, j: (0, 0, 0)),
                  pl.BlockSpec((None,) + ar.shape[1:], lambda d, j: (d, 0, 0, 0)),
                  pl.BlockSpec((None,) + ai.shape[1:], lambda d, j: (d, 0, 0, 0))],
        out_specs=pl.BlockSpec((None, tm, width), lambda d, j: (d, chunk(d, j), 0)),
        out_shape=jax.ShapeDtypeStruct((2, rows, width), F32),
        scratch_shapes=[pltpu.VMEM((tm, 2 * S5_LANES), F32), pltpu.VMEM((tm, 2 * S5_LANES), F32),
                        pltpu.VMEM((SUBLANES, 2 * S5_LANES), F32)],
        compiler_params=_cparams(("arbitrary", "arbitrary")),
        name="s5_scan",
    )(us, us, bd, cd, ar, ai)


def _s5_disc_kernel(lr_ref, li_ref, ldt_ref, ar_ref, ai_ref, cr_ref, ci_ref):
    lr = lr_ref[...]
    li = li_ref[...]
    dt = jnp.exp(ldt_ref[...])
    mag = jnp.exp(lr * dt)
    ar = mag * jnp.cos(li * dt)
    ai = mag * jnp.sin(li * dt)
    den = lr * lr + li * li
    nr = ar - 1.0
    ar_ref[...] = ar
    ai_ref[...] = ai
    cr_ref[...] = (nr * lr + ai * li) / den
    ci_ref[...] = (ai * lr - nr * li) / den


def _s5_discretise(lam_re, lam_im, log_dt):
    n = lam_re.shape[0]
    shp = jax.ShapeDtypeStruct((n, S5_STATE), F32)
    return pl.pallas_call(_s5_disc_kernel, out_shape=[shp] * 4, name="s5_discretise")(
        lam_re, lam_im, jnp.broadcast_to(log_dt, (n, S5_STATE)))


def _gelu_tanh(y):
    return 0.5 * y * (1.0 + jnp.tanh(math.sqrt(2.0 / math.pi) * (y + 0.044715 * (y * y * y))))


def _merge_kernel(x_ref, mod_ref, ya_ref, hl_ref, ys_ref, us_ref, za_ref, zr_ref, zs_ref, gl_ref,
                  sd_ref, wglu_ref, bglu_ref, wbr_ref, wout_ref, fg_ref, o_ref, slab_scr, out_scr):
    tm, d = x_ref.shape
    y = ys_ref[0] + ys_ref[1] + sd_ref[...] * us_ref[...].astype(F32)
    g = _gelu_tanh(y)
    hg = 0.5 * g
    ys = hg + hg * jnp.tanh(jnp.dot(g.astype(BF16), wglu_ref[...], preferred_element_type=F32)
                            + bglu_ref[...])
    yr = hl_ref[0] + hl_ref[1]
    for hh in range(ATT_HEADS):
        for b in range(SUBLANES):
            slab_scr[hh, pl.ds(b, tm // SUBLANES, stride=SUBLANES), :] = (
                ya_ref[b, :, hh * LANES:(hh + 1) * LANES].astype(F32))
    ya = jnp.concatenate([slab_scr[hh] for hh in range(ATT_HEADS)], axis=1)
    acc = jnp.zeros((tm, d), F32)
    for n, (yn, zn_ref) in enumerate(((ya, za_ref), (yr, zr_ref), (ys, zs_ref))):
        zh = zn_ref[...].astype(F32)
        yz = yn * zh
        hyb = jnp.dot((yz + yz * jnp.tanh(zh)).astype(BF16), wbr_ref[n], preferred_element_type=F32)
        acc = acc + (hyb + hyb * jnp.tanh(gl_ref[:, n * d:(n + 1) * d].astype(F32)))
    upd = jnp.dot(acc.astype(BF16), wout_ref[...], preferred_element_type=F32)
    split = lambda a: a.reshape(tm // SUBLANES, SUBLANES, d)
    xn = (split(x_ref[...]) + mod_ref[2][None] * split(upd)).reshape(tm, d)
    if fg_ref is None:
        o_ref[...] = xn
    else:
        y = xn * lax.rsqrt(jnp.mean(xn * xn, axis=-1, keepdims=True) + EPS) * fg_ref[...]
        for s in range(d // LANES):
            out_scr[s] = y[:, s * LANES:(s + 1) * LANES]
            for b in range(SUBLANES):
                o_ref[b, :, s * LANES:(s + 1) * LANES] = out_scr[s, pl.ds(b, tm // SUBLANES, stride=SUBLANES), :]


def _merge_mid_kernel(*refs):
    _merge_kernel(*refs[:15], None, refs[15], refs[16], None)


def _merge(x, mod, ya, hl, ys, us, za, zr, zs, gl, s5_d, w_glu, b_glu, w_branch, w_out, n_ctx_chunks,
           final_g=None):
    rows, d = x.shape
    tm = CHUNK_T * SUBLANES
    nch = rows // tm
    w = BRANCH_W
    last = final_g is not None
    first = n_ctx_chunks if last else 0
    row_spec = lambda n: pl.BlockSpec((tm, n), lambda i: (i + first, 0))
    pair_spec = pl.BlockSpec((2, tm, w), lambda i: (0, i + first, 0))
    full = lambda a: pl.BlockSpec(a.shape, lambda i: (0,) * a.ndim)
    in_specs = [row_spec(d),
                pl.BlockSpec((None, 3, SUBLANES, d),
                             lambda i: (jnp.where(i + first < n_ctx_chunks, 0, 1), 0, 0, 0)),
                pl.BlockSpec((SUBLANES, CHUNK_T, w), lambda i: (0, i + first, 0)),
                pair_spec, pair_spec, row_spec(w), row_spec(w), row_spec(w),
                row_spec(w), row_spec(N_BRANCH * d),
                full(s5_d), full(w_glu), full(b_glu), full(w_branch), full(w_out)]
    args = [x, mod, ya, hl, ys, us, za, zr, zs, gl, s5_d, w_glu, b_glu, w_branch, w_out]
    scratch = [pltpu.VMEM((ATT_HEADS, tm, LANES), F32)]
    if last:
        return pl.pallas_call(
            _merge_kernel,
            grid=(nch - first,),
            in_specs=in_specs + [full(final_g)],
            out_specs=pl.BlockSpec((SUBLANES, CHUNK_T, d), lambda i: (0, i, 0)),
            out_shape=jax.ShapeDtypeStruct((SUBLANES, (nch - first) * CHUNK_T, d), F32),
            scratch_shapes=scratch + [pltpu.VMEM((d // LANES, tm, LANES), F32)],
            compiler_params=_cparams(("parallel",)),
            name="gated_merge_final",
        )(*args, final_g)
    return pl.pallas_call(
        _merge_mid_kernel,
        grid=(nch,),
        in_specs=in_specs,
        out_specs=row_spec(d),
        out_shape=jax.ShapeDtypeStruct((rows, d), F32),
        scratch_shapes=scratch,
        compiler_params=_cparams(("parallel",)),
        name="gated_merge",
    )(*args)


def _rope_tables(n_lat, n_ctx, batch):
    n_freq = ATT_QK_DIM // 4
    tl = jnp.arange(n_lat)
    inv = ROPE_BASE ** (-jnp.arange(n_freq, dtype=F32) / n_freq)
    ang = jnp.concatenate([(tl // GRID_W).astype(F32)[:, None] * inv,
                           (tl % GRID_W).astype(F32)[:, None] * inv], axis=-1)
    cos = jnp.tile(jnp.repeat(jnp.cos(ang), 2, axis=1), (1, 2))
    sin = jnp.tile(jnp.repeat(jnp.sin(ang), 2, axis=1), (1, 2)) * jnp.where(jnp.arange(LANES) % 2 == 0, -1.0, 1.0)
    cos = jnp.concatenate([jnp.ones((n_ctx, LANES), F32), cos], axis=0)
    sin = jnp.concatenate([jnp.zeros((n_ctx, LANES), F32), sin], axis=0)
    rep = lambda a: jnp.broadcast_to(a[:, None, :], (a.shape[0], batch, LANES)).reshape(-1, LANES)
    return rep(cos), rep(sin)


def _block_diag(w):
    *lead, n, c, _ = w.shape
    dense = w[..., :, :, None, :] * jnp.eye(n, dtype=w.dtype)[:, None, :, None]
    return dense.reshape(*lead, n * c, n * c)


def kernel(x, c, ctx, c_ctx, w_mod, b_mod, norm_g, w_in, lam_qk, subln_g, conv_w, conv_b, lru_wa, lru_ba,
           lru_wx, lru_bx, lru_lam, s5_lam_re, s5_lam_im, s5_log_dt, s5_b_re, s5_b_im, s5_c_re, s5_c_im,
           s5_d, s5_w_glu, s5_b_glu, w_branch, w_out, final_g):
    batch, n_lat, d = x.shape
    n_ctx = ctx.shape[1]
    depth = w_mod.shape[0]
    assert batch == SUBLANES and n_lat % CHUNK_T == 0 and n_ctx % CHUNK_T == 0
    n_ctx_chunks = n_ctx // CHUNK_T
    s_all = n_ctx + n_lat
    rows = s_all * batch

    xs = jnp.concatenate([jnp.swapaxes(ctx, 0, 1), jnp.swapaxes(x, 0, 1)], axis=0).reshape(rows, d)

    cond = jnp.zeros((2 * SUBLANES, d), F32).at[:batch].set(c).at[batch].set(c_ctx)
    m = _ada_mod(cond, w_mod, b_mod).reshape(depth, 2 * SUBLANES, 3, d)
    mod = jnp.stack([jnp.broadcast_to(m[:, batch][:, None], (depth, batch, 3, d)), m[:, :batch]], axis=1)
    mod = jnp.swapaxes(mod, 2, 3)

    cos, sin = _rope_tables(n_lat, n_ctx, batch)
    seg = jnp.arange(w_in.shape[-1]) // BRANCH_W
    halved = (seg == 3) | (seg == 5) | (seg >= 7)
    w_in_p = (w_in * jnp.where(halved, 0.5, 1.0)).astype(BF16)

    n_dir = depth * 2 * S5_GROUPS
    ar, ai, cr, ci = _s5_discretise(s5_lam_re.reshape(n_dir, S5_STATE), s5_lam_im.reshape(n_dir, S5_STATE),
                                    s5_log_dt.reshape(n_dir, 1))
    gshape = (depth, 2, S5_GROUPS, S5_STATE)
    ar, ai, cr, ci = (a.reshape(gshape) for a in (ar, ai, cr, ci))
    bbr = cr[..., None] * s5_b_re - ci[..., None] * s5_b_im
    bbi = cr[..., None] * s5_b_im + ci[..., None] * s5_b_re
    nb, bg = S5_BLOCKS, S5_BLOCK_GROUPS
    eye_g = jnp.eye(bg, dtype=F32)

    def drive(bb):
        bb = bb.reshape(depth, 2, nb, bg, S5_STATE, S5_GROUP)
        return (bb.transpose(0, 1, 2, 3, 5, 4)[:, :, :, :, :, None, :]
                * eye_g[:, None, :, None]).reshape(depth, 2, nb, bg * S5_GROUP, S5_HALF)

    def read(cc):
        cc = cc.reshape(depth, nb, bg, S5_GROUP, S5_STATE)
        return (cc.transpose(0, 1, 2, 4, 3)[:, :, :, :, None, :]
                * eye_g[:, None, :, None]).reshape(depth, nb, S5_HALF, bg * S5_GROUP)

    bd = jnp.concatenate([drive(bbr), drive(bbi)], axis=-1).astype(BF16)
    cd = jnp.concatenate([read(s5_c_re), -read(s5_c_im)], axis=2).astype(BF16)
    ar = ar.reshape(depth, 2, nb, 1, S5_HALF)
    ai = ai.reshape(depth, 2, nb, 1, S5_HALF)

    w_gate = (0.5 * jnp.concatenate([_block_diag(lru_wa), _block_diag(lru_wx)], axis=-1)).astype(BF16)
    b_gate = 0.5 * jnp.concatenate([lru_ba, lru_bx], axis=-1)[:, :, None, :]
    w_glu_h = (0.5 * s5_w_glu).astype(BF16)
    b_glu_h = 0.5 * s5_b_glu
    w_branch_h = (0.5 * w_branch).astype(BF16)
    w_out_b = w_out.astype(BF16)

    for l in range(depth):
        lam_init = 0.8 - 0.6 * math.exp(-0.3 * l)
        q, k, v, za, xr, zr, us, zs, gl = _in_proj(xs, norm_g[l][None], mod[l], cos, sin, w_in_p[l],
                                                   n_ctx_chunks)
        ya = _attention(q, k, v, lam_qk[l], subln_g[l][None], lam_init, n_ctx)
        hl = _rglru(xr, conv_w[l], conv_b[l][None], w_gate[l], b_gate[l], lru_lam[l][:, None, :],
                    n_ctx_chunks)
        ys = _s5(us, bd[l], cd[l], ar[l], ai[l], n_ctx_chunks)
        xs = _merge(xs, mod[l], ya, hl, ys, us, za, zr, zs, gl, s5_d[l][None], w_glu_h[l],
                    b_glu_h[l][None], w_branch_h[l], w_out_b[l], n_ctx_chunks,
                    final_g=final_g[None] if l == depth - 1 else None)
    return xs
```

```python
import functools
import math

import jax
import jax.numpy as jnp
from jax import lax
from jax.experimental import pallas as pl
from jax.experimental.pallas import tpu as pltpu

F32 = jnp.float32
BF16 = jnp.bfloat16

EPS = 1e-6
GRID_W = 64
ATT_HEADS = 4
ATT_QK_DIM = 64
ATT_V_DIM = 2 * ATT_QK_DIM
BRANCH_W = 512
N_BRANCH = 3
ROPE_BASE = 10000.0
LRU_BLOCKS = 8
LRU_C = 8.0
CONV_W = 4
S5_GROUP = 16
S5_GROUPS = BRANCH_W // S5_GROUP
S5_STATE = 64
S5_LANES = S5_GROUPS * S5_STATE
S5_BLOCK_GROUPS = 8
S5_BLOCKS = S5_GROUPS // S5_BLOCK_GROUPS
S5_HALF = S5_BLOCK_GROUPS * S5_STATE

SUBLANES = 8
LANES = 128
CHUNK_T = 64
KEY_CHUNK = 256
ATT_TQ = 512
VMEM_LIMIT = 56 * 1024 * 1024

LOG2E = math.log2(math.e)


def _cparams(sem):
    return pltpu.CompilerParams(dimension_semantics=sem, vmem_limit_bytes=VMEM_LIMIT)


def _sigmoid(z):
    return 1.0 / (1.0 + jnp.exp(-z))


def _silu(z):
    return z * _sigmoid(z)


def _mod_kernel(c_ref, w_ref, b_ref, o_ref):
    c = c_ref[...]
    o_ref[...] = jnp.dot(_silu(c), w_ref[...], preferred_element_type=F32,
                         precision=lax.Precision.HIGHEST) + b_ref[...]


def _ada_mod(cond, w_mod, b_mod):
    depth, d, d3 = w_mod.shape
    return pl.pallas_call(
        _mod_kernel,
        grid=(depth, d3 // d),
        in_specs=[pl.BlockSpec((2 * SUBLANES, d), lambda l, j: (0, 0)),
                  pl.BlockSpec((None, d, d), lambda l, j: (l, 0, j)),
                  pl.BlockSpec((None, 1, d), lambda l, j: (l, 0, j))],
        out_specs=pl.BlockSpec((None, 2 * SUBLANES, d), lambda l, j: (l, 0, j)),
        out_shape=jax.ShapeDtypeStruct((depth, 2 * SUBLANES, d3), F32),
        compiler_params=_cparams(("parallel", "parallel")),
        name="ada_mod",
    )(cond, w_mod, b_mod.reshape(depth, 1, d3))


def _rows_kernel(ctx_ref, x_ref, o_ref, slab_scr, *, n_ctx_chunks):
    i = pl.program_id(0)
    nt = ctx_ref.shape[1]

    def relayout(src_ref):
        for s in range(o_ref.shape[1] // LANES):
            for b in range(SUBLANES):
                slab_scr[s, pl.ds(b, nt, stride=SUBLANES), :] = src_ref[b, :, s * LANES:(s + 1) * LANES]
            o_ref[:, s * LANES:(s + 1) * LANES] = slab_scr[s]

    @pl.when(i < n_ctx_chunks)
    def _():
        relayout(ctx_ref)

    @pl.when(i >= n_ctx_chunks)
    def _():
        relayout(x_ref)


def _token_rows(ctx, x):
    batch, n_ctx, d = ctx.shape
    n_ctx_chunks = n_ctx // CHUNK_T
    nch = n_ctx_chunks + x.shape[1] // CHUNK_T
    tm = CHUNK_T * SUBLANES
    return pl.pallas_call(
        functools.partial(_rows_kernel, n_ctx_chunks=n_ctx_chunks),
        grid=(nch,),
        in_specs=[pl.BlockSpec((SUBLANES, CHUNK_T, d), lambda i: (0, jnp.minimum(i, n_ctx_chunks - 1), 0)),
                  pl.BlockSpec((SUBLANES, CHUNK_T, d), lambda i: (0, jnp.maximum(i - n_ctx_chunks, 0), 0))],
        out_specs=pl.BlockSpec((tm, d), lambda i: (i, 0)),
        out_shape=jax.ShapeDtypeStruct((nch * tm, d), F32),
        scratch_shapes=[pltpu.VMEM((d // LANES, tm, LANES), F32)],
        compiler_params=_cparams(("arbitrary",)),
        name="token_rows",
    )(ctx, x)


def _inproj_kernel(x_ref, g_ref, mod_ref, cos_ref, sin_ref, w_ref,
                   q_ref, k_ref, v_ref, za_ref, xr_ref, zr_ref, us_ref, zs_ref, gl_ref,
                   h_scr, slab_scr, *, q_scale):
    x = x_ref[...]
    tm, d = x.shape
    nt = tm // SUBLANES
    y = x * lax.rsqrt(jnp.mean(x * x, axis=-1, keepdims=True) + EPS) * g_ref[...]
    y = y.reshape(nt, SUBLANES, d)
    h = y * (1.0 + mod_ref[1])[None] + mod_ref[0][None]
    h_scr[...] = h.reshape(tm, d).astype(BF16)

    def proj(j):
        return jnp.dot(h_scr[...], w_ref[:, j * BRANCH_W:(j + 1) * BRANCH_W],
                       preferred_element_type=F32)

    cos = cos_ref[...]
    sin = sin_ref[...]
    even = lax.broadcasted_iota(jnp.int32, cos.shape, 1) % 2 == 0

    def to_batch_major(a, ref, slab0, rope, scale):
        for hh in range(ATT_HEADS):
            r = a[:, hh * LANES:(hh + 1) * LANES]
            if rope:
                partner = jnp.where(even, pltpu.roll(r, LANES - 1, axis=1), pltpu.roll(r, 1, axis=1))
                r = (r * cos + partner * sin) * scale
            slab_scr[slab0 + hh] = r
            for b in range(SUBLANES):
                ref[b, :, hh * LANES:(hh + 1) * LANES] = (
                    slab_scr[slab0 + hh, pl.ds(b, nt, stride=SUBLANES), :].astype(BF16))

    to_batch_major(proj(0), q_ref, 0, True, q_scale)
    to_batch_major(proj(1), k_ref, ATT_HEADS, True, 1.0)
    to_batch_major(proj(2), v_ref, 2 * ATT_HEADS, False, 1.0)
    for j, ref in ((3, za_ref), (4, xr_ref), (5, zr_ref), (6, us_ref), (7, zs_ref)):
        ref[...] = proj(j).astype(BF16)
    for j in range(2 * N_BRANCH):
        gl_ref[:, j * BRANCH_W:(j + 1) * BRANCH_W] = proj(8 + j).astype(BF16)


def _att_chunk(i, n_ctx_chunks, n_pad_chunks):
    return jnp.where(i < n_ctx_chunks, i, i + n_pad_chunks)


def _in_proj(x, norm_g, mod, cos, sin, w_in, n_ctx_chunks, n_pad_chunks):
    rows, d = x.shape
    tm = CHUNK_T * SUBLANES
    nch = rows // tm
    row_spec = lambda w: pl.BlockSpec((tm, w), lambda i: (i, 0))
    out_w = [BRANCH_W] * 5 + [N_BRANCH * d]
    bm_spec = pl.BlockSpec((SUBLANES, CHUNK_T, BRANCH_W),
                           lambda i: (0, _att_chunk(i, n_ctx_chunks, n_pad_chunks), 0))
    bm_shape = jax.ShapeDtypeStruct((SUBLANES, (nch + n_pad_chunks) * CHUNK_T, BRANCH_W), BF16)
    return pl.pallas_call(
        functools.partial(_inproj_kernel, q_scale=ATT_QK_DIM ** -0.5 * LOG2E),
        grid=(nch,),
        in_specs=[row_spec(d),
                  pl.BlockSpec((1, d), lambda i: (0, 0)),
                  pl.BlockSpec((None, 3, SUBLANES, d),
                               lambda i: (jnp.where(i < n_ctx_chunks, 0, 1), 0, 0, 0)),
                  row_spec(LANES), row_spec(LANES),
                  pl.BlockSpec(w_in.shape, lambda i: (0, 0), pipeline_mode=pl.Buffered(1))],
        out_specs=[bm_spec] * 3 + [row_spec(w) for w in out_w],
        out_shape=[bm_shape] * 3 + [jax.ShapeDtypeStruct((rows, w), BF16) for w in out_w],
        scratch_shapes=[pltpu.VMEM((tm, d), BF16), pltpu.VMEM((3 * ATT_HEADS, tm, LANES), F32)],
        compiler_params=_cparams(("parallel",)),
        name="in_proj",
    )(x, norm_g, mod, cos, sin, w_in)


def _attn_kernel(lamqk_ref, g_ref, qc_ref, qn_ref, k_ref, v_ref, o_ref,
                 s0_scr, s1_scr, m0_scr, m1_scr, a0_scr, a1_scr, vext_scr, *,
                 tq, n_ctx_tiles, n_tiles, n_ctx, n_all, n_rows, lam_init):
    i = pl.program_id(2)

    @pl.when(i == 0)
    def _():
        for off in range(0, n_rows, LANES):
            vext_scr[:ATT_V_DIM, off:off + LANES] = v_ref[off:off + LANES, :].astype(F32).T.astype(BF16)
        vext_scr[ATT_V_DIM:, :] = jnp.ones((2 * SUBLANES, n_rows), BF16)

    lq = lamqk_ref[...]
    lam = (jnp.exp(jnp.sum(lq[0:1] * lq[1:2], axis=-1, keepdims=True))
           - jnp.exp(jnp.sum(lq[2:3] * lq[3:4], axis=-1, keepdims=True)) + lam_init)

    def chunks(nk):
        kch = math.gcd(KEY_CHUNK, n_ctx)
        return [(boff if boff < n_ctx else boff + (n_rows - n_all), kch, boff) for boff in range(0, nk, kch)]

    cols = (slice(0, tq), slice(tq, 2 * tq))

    def split(q_ref):
        qt = q_ref[...].astype(F32).T
        row = lax.broadcasted_iota(jnp.int32, qt.shape, 0)
        comp0 = row < ATT_QK_DIM
        return (jnp.where(comp0, qt, 0.0).astype(BF16), jnp.where(comp0, 0.0, qt).astype(BF16))

    def score_chunk(qs, c, off, kc, boff, s_dst, mm):
        st = jnp.dot(k_ref[off:off + kc, :], qs[c], preferred_element_type=F32)
        s_dst[boff:boff + kc, cols[c]] = st
        part = jnp.max(st.reshape(kc // SUBLANES, SUBLANES, tq), axis=0)
        return part if mm is None else jnp.maximum(mm, part)

    def score_done(c, m_dst, mm):
        m_dst[:, cols[c]] = jnp.broadcast_to(jnp.max(mm, axis=0, keepdims=True), (SUBLANES, tq))

    def weight_chunk(c, off, kc, boff, s_src, m, acc):
        st = s_src[boff:boff + kc, cols[c]].reshape(kc // SUBLANES, SUBLANES, tq)
        pt = jnp.exp2(st - m[None]).reshape(kc, tq).astype(BF16)
        return acc + jnp.dot(vext_scr[:, off:off + kc], pt, preferred_element_type=F32)

    def epilogue(acc_src):
        ots = [acc_src[c, :ATT_V_DIM] / acc_src[c, ATT_V_DIM:ATT_V_DIM + 1] for c in range(2)]
        o = (ots[0] - lam * ots[1]).T
        o = o * lax.rsqrt(jnp.mean(o * o, axis=-1, keepdims=True) + EPS) * g_ref[...]
        o_ref[...] = (o * (1.0 - lam_init)).astype(BF16)

    acc0 = jnp.zeros((ATT_V_DIM + 2 * SUBLANES, tq), F32)

    def scores(q_ref, nk, s_dst, m_dst):
        qs = split(q_ref)
        for c in range(2):
            mm = None
            for ch in chunks(nk):
                mm = score_chunk(qs, c, *ch, s_dst, mm)
            score_done(c, m_dst, mm)

    def weighted(nk, s_src, m_src, acc_dst):
        for c in range(2):
            acc = acc0
            for ch in chunks(nk):
                acc = weight_chunk(c, *ch, s_src, m_src[:, cols[c]], acc)
            acc_dst[c] = acc

    def both(nk, s_src, m_src, acc_dst, q_ref, s_dst, m_dst):
        qs = split(q_ref)
        ms = [m_src[:, cols[c]] for c in range(2)]
        accs = [acc0, acc0]
        mms = [None, None]
        for ch in chunks(nk):
            for c in range(2):
                accs[c] = weight_chunk(c, *ch, s_src, ms[c], accs[c])
                mms[c] = score_chunk(qs, c, *ch, s_dst, mms[c])
        for c in range(2):
            acc_dst[c] = accs[c]
            score_done(c, m_dst, mms[c])

    slots = ((s0_scr, m0_scr, a0_scr), (s1_scr, m1_scr, a1_scr))
    for par in range(2):
        cur, nxt = slots[par], slots[1 - par]
        mine = i % 2 == par

        @pl.when(jnp.logical_and(mine, i == 0))
        def _(cur=cur):
            scores(qc_ref, n_ctx, *cur[:2])
            weighted(n_ctx, *cur)

        if n_ctx_tiles > 1:
            @pl.when(jnp.logical_and(mine, jnp.logical_and(i > 0, i < n_ctx_tiles)))
            def _(cur=cur, nxt=nxt):
                epilogue(nxt[2])
                scores(qc_ref, n_ctx, *cur[:2])
                weighted(n_ctx, *cur)

        @pl.when(jnp.logical_and(mine, i == n_ctx_tiles - 1))
        def _(nxt=nxt):
            scores(qn_ref, n_all, *nxt[:2])

        @pl.when(jnp.logical_and(mine, jnp.logical_and(i >= n_ctx_tiles, i < n_tiles)))
        def _(cur=cur, nxt=nxt):
            epilogue(nxt[2])
            both(n_all, *cur, qn_ref, *nxt[:2])

        @pl.when(jnp.logical_and(mine, i == n_tiles))
        def _(nxt=nxt):
            epilogue(nxt[2])


def _attention(q, k, v, lam_qk, subln_g, lam_init, n_ctx, n_pad, tq):
    batch, n_rows, width = q.shape
    s_all = n_rows - n_pad
    nt = n_rows // tq
    acc_shape = (2, ATT_V_DIM + 2 * SUBLANES, tq)
    return pl.pallas_call(
        functools.partial(_attn_kernel, tq=tq, n_ctx_tiles=(n_ctx + n_pad) // tq, n_tiles=nt, n_ctx=n_ctx,
                          n_all=s_all, n_rows=n_rows, lam_init=lam_init),
        grid=(batch, ATT_HEADS, nt + 1),
        in_specs=[pl.BlockSpec((4, ATT_QK_DIM), lambda b, h, i: (0, 0)),
                  pl.BlockSpec((1, ATT_V_DIM), lambda b, h, i: (0, 0)),
                  pl.BlockSpec((None, tq, LANES), lambda b, h, i: (b, jnp.minimum(i, nt - 1), h)),
                  pl.BlockSpec((None, tq, LANES), lambda b, h, i: (b, jnp.minimum(i + 1, nt - 1), h)),
                  pl.BlockSpec((None, n_rows, LANES), lambda b, h, i: (b, 0, h)),
                  pl.BlockSpec((None, n_rows, LANES), lambda b, h, i: (b, 0, h))],
        out_specs=pl.BlockSpec((None, tq, LANES), lambda b, h, i: (b, jnp.maximum(i - 1, 0), h)),
        out_shape=jax.ShapeDtypeStruct((batch, n_rows, width), BF16),
        scratch_shapes=[pltpu.VMEM((s_all, 2 * tq), F32), pltpu.VMEM((s_all, 2 * tq), F32),
                        pltpu.VMEM((SUBLANES, 2 * tq), F32), pltpu.VMEM((SUBLANES, 2 * tq), F32),
                        pltpu.VMEM(acc_shape, F32), pltpu.VMEM(acc_shape, F32),
                        pltpu.VMEM((ATT_V_DIM + 2 * SUBLANES, n_rows), BF16)],
        compiler_params=_cparams(("parallel", "parallel", "arbitrary")),
        name="diff_attention",
    )(lam_qk, subln_g, q, q, k, v)


def _scan_chunk(d, j, n_ctx_chunks, nch):
    back = jnp.where(j < n_ctx_chunks, n_ctx_chunks - 1 - j, nch - 1 + n_ctx_chunks - j)
    return jnp.where(d == 0, j, back)


def _lru_kernel(xp_ref, xc_ref, xn_ref, cw_ref, cb_ref, wg_ref, bg_ref, lam_ref, o_ref,
                a_scr, b_scr, h_scr, *, n_ctx_chunks, nch):
    d = pl.program_id(0)
    j = pl.program_id(1)
    c = _scan_chunk(d, j, n_ctx_chunks, nch)
    tm = xc_ref.shape[0]
    first = jnp.logical_or(c == 0, c == n_ctx_chunks)
    last = jnp.logical_or(c == n_ctx_chunks - 1, c == nch - 1)
    xp = xp_ref[...].astype(F32) * jnp.where(first, 0.0, 1.0)
    xn = xn_ref[...].astype(F32)[:SUBLANES] * jnp.where(last, 0.0, 1.0)
    xe = jnp.concatenate([xp, xc_ref[...].astype(F32), xn], axis=0)
    u = cb_ref[...]
    for tap in range(CONV_W):
        u = u + cw_ref[tap:tap + 1, :] * xe[tap * SUBLANES:tap * SUBLANES + tm]
    gates = jnp.tanh(jnp.dot(u.astype(BF16), wg_ref[...], preferred_element_type=F32) + bg_ref[...])
    z = -lam_ref[...]
    softplus = jnp.maximum(z, 0.0) + jnp.log(1.0 + jnp.exp(-jnp.abs(z)))
    half_c = (-0.5 * LRU_C * LOG2E) * softplus
    a = jnp.exp2(half_c + half_c * gates[:, :BRANCH_W])
    half_u = 0.5 * u
    a_scr[...] = a
    b_scr[...] = jnp.exp2(0.5 * jnp.log2(1.0 - a * a)) * (half_u + half_u * gates[:, BRANCH_W:])

    @pl.when(j == 0)
    def _():
        h_scr[...] = jnp.zeros(h_scr.shape, F32)

    nt = tm // SUBLANES

    def step(t, h):
        te = jnp.where(d == 0, t, nt - 1 - t)
        r0 = pl.multiple_of(te * SUBLANES, SUBLANES)
        h = a_scr[pl.ds(r0, SUBLANES), :] * h + b_scr[pl.ds(r0, SUBLANES), :]
        o_ref[pl.ds(r0, SUBLANES), :] = h
        return h

    h_scr[...] = lax.fori_loop(0, nt, step, h_scr[...], unroll=8)


def _rglru(xr, conv_w, conv_b, w_gate, b_gate, lru_lam, n_ctx_chunks):
    rows, width = xr.shape
    tm = CHUNK_T * SUBLANES
    nch = rows // tm
    halo = 2 * SUBLANES
    per = tm // halo
    chunk = lambda d, j: _scan_chunk(d, j, n_ctx_chunks, nch)
    return pl.pallas_call(
        functools.partial(_lru_kernel, n_ctx_chunks=n_ctx_chunks, nch=nch),
        grid=(2, nch),
        in_specs=[pl.BlockSpec((halo, width), lambda d, j: (jnp.maximum(chunk(d, j) * per - 1, 0), 0)),
                  pl.BlockSpec((tm, width), lambda d, j: (chunk(d, j), 0)),
                  pl.BlockSpec((halo, width),
                               lambda d, j: (jnp.minimum((chunk(d, j) + 1) * per, nch * per - 1), 0)),
                  pl.BlockSpec((CONV_W, width), lambda d, j: (0, 0)),
                  pl.BlockSpec((1, width), lambda d, j: (0, 0)),
                  pl.BlockSpec((None, width, 2 * width), lambda d, j: (d, 0, 0)),
                  pl.BlockSpec((None, 1, 2 * width), lambda d, j: (d, 0, 0)),
                  pl.BlockSpec((None, 1, width), lambda d, j: (d, 0, 0))],
        out_specs=pl.BlockSpec((None, tm, width), lambda d, j: (d, chunk(d, j), 0)),
        out_shape=jax.ShapeDtypeStruct((2, rows, width), F32),
        scratch_shapes=[pltpu.VMEM((tm, width), F32), pltpu.VMEM((tm, width), F32),
                        pltpu.VMEM((SUBLANES, width), F32)],
        compiler_params=_cparams(("arbitrary", "arbitrary")),
        name="rglru",
    )(xr, xr, xr, conv_w, conv_b, w_gate, b_gate, lru_lam)


def _s5_kernel(us_ref, usn_ref, bd_ref, cd_ref, ar_ref, ai_ref, o_ref, drv0_scr, drv1_scr, h_scr):
    d = pl.program_id(0)
    j = pl.program_id(1)
    tm = us_ref.shape[0]
    nt = tm // SUBLANES
    bw = 2 * S5_HALF

    def drive(src_ref, dst, jj):
        dst[:, jj * bw:(jj + 1) * bw] = jnp.dot(src_ref[:, jj * LANES:(jj + 1) * LANES], bd_ref[jj],
                                                preferred_element_type=F32)

    @pl.when(j == 0)
    def _():
        h_scr[...] = jnp.zeros(h_scr.shape, F32)
        for jj in range(S5_BLOCKS):
            drive(us_ref, drv0_scr, jj)

    def scan(buf, jj, order):
        re = slice(jj * bw, jj * bw + S5_HALF)
        im = slice(jj * bw + S5_HALF, (jj + 1) * bw)
        ar = jnp.broadcast_to(ar_ref[jj], (SUBLANES, S5_HALF))
        ai = jnp.broadcast_to(ai_ref[jj], (SUBLANES, S5_HALF))
        hr = h_scr[:, re]
        hi = h_scr[:, im]
        for t in order:
            rws = slice(t * SUBLANES, (t + 1) * SUBLANES)
            hr, hi = (ar * hr - ai * hi + buf[rws, re], ar * hi + ai * hr + buf[rws, im])
            buf[rws, re] = hr
            buf[rws, im] = hi
        h_scr[:, re] = hr
        h_scr[:, im] = hi

    def readout(buf, jj):
        o_ref[:, jj * LANES:(jj + 1) * LANES] = jnp.dot(buf[:, jj * bw:(jj + 1) * bw].astype(BF16),
                                                        cd_ref[jj], preferred_element_type=F32)

    def run(cur, nxt, order):
        for stage in range(S5_BLOCKS + 1):
            if stage < S5_BLOCKS:
                scan(cur, stage, order)
                drive(usn_ref, nxt, stage)
            if stage >= 1:
                readout(cur, stage - 1)

    bufs = (drv0_scr, drv1_scr)
    for rev in range(2):
        order = range(nt - 1, -1, -1) if rev else range(nt)
        for par in range(2):
            @pl.when(jnp.logical_and(d == rev, j % 2 == par))
            def _(par=par, order=order):
                run(bufs[par], bufs[1 - par], order)


def _s5(us, bd, cd, ar, ai, n_ctx_chunks):
    rows, width = us.shape
    tm = CHUNK_T * SUBLANES
    nch = rows // tm
    chunk = lambda d, j: _scan_chunk(d, j, n_ctx_chunks, nch)
    return pl.pallas_call(
        _s5_kernel,
        grid=(2, nch),
        in_specs=[pl.BlockSpec((tm, width), lambda d, j: (chunk(d, j), 0)),
                  pl.BlockSpec((tm, width), lambda d, j: (chunk(d, jnp.minimum(j + 1, nch - 1)), 0)),
                  pl.BlockSpec((None,) + bd.shape[1:], lambda d, j: (d, 0, 0, 0)),
                  pl.BlockSpec(cd.shape, lambda d, j: (0, 0, 0)),
                  pl.BlockSpec((None,) + ar.shape[1:], lambda d, j: (d, 0, 0, 0)),
                  pl.BlockSpec((None,) + ai.shape[1:], lambda d, j: (d, 0, 0, 0))],
        out_specs=pl.BlockSpec((None, tm, width), lambda d, j: (d, chunk(d, j), 0)),
        out_shape=jax.ShapeDtypeStruct((2, rows, width), F32),
        scratch_shapes=[pltpu.VMEM((tm, 2 * S5_LANES), F32), pltpu.VMEM((tm, 2 * S5_LANES), F32),
                        pltpu.VMEM((SUBLANES, 2 * S5_LANES), F32)],
        compiler_params=_cparams(("arbitrary", "arbitrary")),
        name="s5_scan",
    )(us, us, bd, cd, ar, ai)


def _s5_disc_kernel(lr_ref, li_ref, ldt_ref, ar_ref, ai_ref, cr_ref, ci_ref):
    lr = lr_ref[...]
    li = li_ref[...]
    dt = jnp.exp(ldt_ref[...])
    mag = jnp.exp(lr * dt)
    ar = mag * jnp.cos(li * dt)
    ai = mag * jnp.sin(li * dt)
    den = lr * lr + li * li
    nr = ar - 1.0
    ar_ref[...] = ar
    ai_ref[...] = ai
    cr_ref[...] = (nr * lr + ai * li) / den
    ci_ref[...] = (ai * lr - nr * li) / den


def _s5_discretise(lam_re, lam_im, log_dt):
    n = lam_re.shape[0]
    shp = jax.ShapeDtypeStruct((n, S5_STATE), F32)
    return pl.pallas_call(_s5_disc_kernel, out_shape=[shp] * 4, name="s5_discretise")(
        lam_re, lam_im, jnp.broadcast_to(log_dt, (n, S5_STATE)))


def _gelu_tanh(y):
    return 0.5 * y * (1.0 + jnp.tanh(math.sqrt(2.0 / math.pi) * (y + 0.044715 * (y * y * y))))


def _merge_kernel(x_ref, mod_ref, ya_ref, hl_ref, ys_ref, us_ref, za_ref, zr_ref, zs_ref, gl_ref,
                  sd_ref, wglu_ref, bglu_ref, wbr_ref, wout_ref, fg_ref, o_ref, slab_scr, out_scr):
    tm, d = x_ref.shape
    y = ys_ref[0] + ys_ref[1] + sd_ref[...] * us_ref[...].astype(F32)
    g = _gelu_tanh(y)
    hg = 0.5 * g
    ys = hg + hg * jnp.tanh(jnp.dot(g.astype(BF16), wglu_ref[...], preferred_element_type=F32)
                            + bglu_ref[...])
    yr = hl_ref[0] + hl_ref[1]
    for hh in range(ATT_HEADS):
        for b in range(SUBLANES):
            slab_scr[hh, pl.ds(b, tm // SUBLANES, stride=SUBLANES), :] = (
                ya_ref[b, :, hh * LANES:(hh + 1) * LANES].astype(F32))
    ya = jnp.concatenate([slab_scr[hh] for hh in range(ATT_HEADS)], axis=1)
    acc = jnp.zeros((tm, d), F32)
    for n, (yn, zn_ref) in enumerate(((ya, za_ref), (yr, zr_ref), (ys, zs_ref))):
        zh = zn_ref[...].astype(F32)
        yz = yn * zh
        hyb = jnp.dot((yz + yz * jnp.tanh(zh)).astype(BF16), wbr_ref[n], preferred_element_type=F32)
        acc = acc + (hyb + hyb * jnp.tanh(gl_ref[:, n * d:(n + 1) * d].astype(F32)))
    upd = jnp.dot(acc.astype(BF16), wout_ref[...], preferred_element_type=F32)
    split = lambda a: a.reshape(tm // SUBLANES, SUBLANES, d)
    xn = (split(x_ref[...]) + mod_ref[2][None] * split(upd)).reshape(tm, d)
    if fg_ref is None:
        o_ref[...] = xn
    else:
        y = xn * lax.rsqrt(jnp.mean(xn * xn, axis=-1, keepdims=True) + EPS) * fg_ref[...]
        for s in range(d // LANES):
            out_scr[s] = y[:, s * LANES:(s + 1) * LANES]
            for b in range(SUBLANES):
                o_ref[b, :, s * LANES:(s + 1) * LANES] = out_scr[s, pl.ds(b, tm // SUBLANES, stride=SUBLANES), :]


def _merge_mid_kernel(*refs):
    _merge_kernel(*refs[:15], None, refs[15], refs[16], None)


def _merge(x, mod, ya, hl, ys, us, za, zr, zs, gl, s5_d, w_glu, b_glu, w_branch, w_out, n_ctx_chunks,
           n_pad_chunks, final_g=None):
    rows, d = x.shape
    tm = CHUNK_T * SUBLANES
    nch = rows // tm
    w = BRANCH_W
    last = final_g is not None
    first = n_ctx_chunks if last else 0
    row_spec = lambda n: pl.BlockSpec((tm, n), lambda i: (i + first, 0))
    pair_spec = pl.BlockSpec((2, tm, w), lambda i: (0, i + first, 0))
    full = lambda a: pl.BlockSpec(a.shape, lambda i: (0,) * a.ndim)
    in_specs = [row_spec(d),
                pl.BlockSpec((None, 3, SUBLANES, d),
                             lambda i: (jnp.where(i + first < n_ctx_chunks, 0, 1), 0, 0, 0)),
                pl.BlockSpec((SUBLANES, CHUNK_T, w),
                             lambda i: (0, _att_chunk(i + first, n_ctx_chunks, n_pad_chunks), 0)),
                pair_spec, pair_spec, row_spec(w), row_spec(w), row_spec(w),
                row_spec(w), row_spec(N_BRANCH * d),
                full(s5_d), full(w_glu), full(b_glu), full(w_branch), full(w_out)]
    args = [x, mod, ya, hl, ys, us, za, zr, zs, gl, s5_d, w_glu, b_glu, w_branch, w_out]
    scratch = [pltpu.VMEM((ATT_HEADS, tm, LANES), F32)]
    if last:
        return pl.pallas_call(
            _merge_kernel,
            grid=(nch - first,),
            in_specs=in_specs + [full(final_g)],
            out_specs=pl.BlockSpec((SUBLANES, CHUNK_T, d), lambda i: (0, i, 0)),
            out_shape=jax.ShapeDtypeStruct((SUBLANES, (nch - first) * CHUNK_T, d), F32),
            scratch_shapes=scratch + [pltpu.VMEM((d // LANES, tm, LANES), F32)],
            compiler_params=_cparams(("parallel",)),
            name="gated_merge_final",
        )(*args, final_g)
    return pl.pallas_call(
        _merge_mid_kernel,
        grid=(nch,),
        in_specs=in_specs,
        out_specs=row_spec(d),
        out_shape=jax.ShapeDtypeStruct((rows, d), F32),
        scratch_shapes=scratch,
        compiler_params=_cparams(("parallel",)),
        name="gated_merge",
    )(*args)


def _rope_tables(n_lat, n_ctx, batch):
    n_freq = ATT_QK_DIM // 4
    tl = jnp.arange(n_lat)
    inv = ROPE_BASE ** (-jnp.arange(n_freq, dtype=F32) / n_freq)
    ang = jnp.concatenate([(tl // GRID_W).astype(F32)[:, None] * inv,
                           (tl % GRID_W).astype(F32)[:, None] * inv], axis=-1)
    cos = jnp.tile(jnp.repeat(jnp.cos(ang), 2, axis=1), (1, 2))
    sin = jnp.tile(jnp.repeat(jnp.sin(ang), 2, axis=1), (1, 2)) * jnp.where(jnp.arange(LANES) % 2 == 0, -1.0, 1.0)
    cos = jnp.concatenate([jnp.ones((n_ctx, LANES), F32), cos], axis=0)
    sin = jnp.concatenate([jnp.zeros((n_ctx, LANES), F32), sin], axis=0)
    rep = lambda a: jnp.broadcast_to(a[:, None, :], (a.shape[0], batch, LANES)).reshape(-1, LANES)
    return rep(cos), rep(sin)


def _block_diag(w):
    *lead, n, c, _ = w.shape
    dense = w[..., :, :, None, :] * jnp.eye(n, dtype=w.dtype)[:, None, :, None]
    return dense.reshape(*lead, n * c, n * c)


def kernel(x, c, ctx, c_ctx, w_mod, b_mod, norm_g, w_in, lam_qk, subln_g, conv_w, conv_b, lru_wa, lru_ba,
           lru_wx, lru_bx, lru_lam, s5_lam_re, s5_lam_im, s5_log_dt, s5_b_re, s5_b_im, s5_c_re, s5_c_im,
           s5_d, s5_w_glu, s5_b_glu, w_branch, w_out, final_g):
    batch, n_lat, d = x.shape
    n_ctx = ctx.shape[1]
    depth = w_mod.shape[0]
    assert batch == SUBLANES and n_lat % CHUNK_T == 0 and n_ctx % CHUNK_T == 0
    n_ctx_chunks = n_ctx // CHUNK_T
    s_all = n_ctx + n_lat
    rows = s_all * batch
    tq = math.gcd(n_lat, ATT_TQ)
    n_pad = -n_ctx % tq
    n_pad_chunks = n_pad // CHUNK_T

    xs = _token_rows(ctx, x)

    cond = jnp.zeros((2 * SUBLANES, d), F32).at[:batch].set(c).at[batch].set(c_ctx)
    m = _ada_mod(cond, w_mod, b_mod).reshape(depth, 2 * SUBLANES, 3, d)
    mod = jnp.stack([jnp.broadcast_to(m[:, batch][:, None], (depth, batch, 3, d)), m[:, :batch]], axis=1)
    mod = jnp.swapaxes(mod, 2, 3)

    cos, sin = _rope_tables(n_lat, n_ctx, batch)
    seg = jnp.arange(w_in.shape[-1]) // BRANCH_W
    halved = (seg == 3) | (seg == 5) | (seg >= 7)
    w_in_p = (w_in * jnp.where(halved, 0.5, 1.0)).astype(BF16)

    n_dir = depth * 2 * S5_GROUPS
    ar, ai, cr, ci = _s5_discretise(s5_lam_re.reshape(n_dir, S5_STATE), s5_lam_im.reshape(n_dir, S5_STATE),
                                    s5_log_dt.reshape(n_dir, 1))
    gshape = (depth, 2, S5_GROUPS, S5_STATE)
    ar, ai, cr, ci = (a.reshape(gshape) for a in (ar, ai, cr, ci))
    bbr = cr[..., None] * s5_b_re - ci[..., None] * s5_b_im
    bbi = cr[..., None] * s5_b_im + ci[..., None] * s5_b_re
    nb, bg = S5_BLOCKS, S5_BLOCK_GROUPS
    eye_g = jnp.eye(bg, dtype=F32)

    def drive(bb):
        bb = bb.reshape(depth, 2, nb, bg, S5_STATE, S5_GROUP)
        return (bb.transpose(0, 1, 2, 3, 5, 4)[:, :, :, :, :, None, :]
                * eye_g[:, None, :, None]).reshape(depth, 2, nb, bg * S5_GROUP, S5_HALF)

    def read(cc):
        cc = cc.reshape(depth, nb, bg, S5_GROUP, S5_STATE)
        return (cc.transpose(0, 1, 2, 4, 3)[:, :, :, :, None, :]
                * eye_g[:, None, :, None]).reshape(depth, nb, S5_HALF, bg * S5_GROUP)

    bd = jnp.concatenate([drive(bbr), drive(bbi)], axis=-1).astype(BF16)
    cd = jnp.concatenate([read(s5_c_re), -read(s5_c_im)], axis=2).astype(BF16)
    ar = ar.reshape(depth, 2, nb, 1, S5_HALF)
    ai = ai.reshape(depth, 2, nb, 1, S5_HALF)

    w_gate = (0.5 * jnp.concatenate([_block_diag(lru_wa), _block_diag(lru_wx)], axis=-1)).astype(BF16)
    b_gate = 0.5 * jnp.concatenate([lru_ba, lru_bx], axis=-1)[:, :, None, :]
    w_glu_h = (0.5 * s5_w_glu).astype(BF16)
    b_glu_h = 0.5 * s5_b_glu
    w_branch_h = (0.5 * w_branch).astype(BF16)
    w_out_b = w_out.astype(BF16)

    for l in range(depth):
        lam_init = 0.8 - 0.6 * math.exp(-0.3 * l)
        q, k, v, za, xr, zr, us, zs, gl = _in_proj(xs, norm_g[l][None], mod[l], cos, sin, w_in_p[l],
                                                   n_ctx_chunks, n_pad_chunks)
        ya = _attention(q, k, v, lam_qk[l], subln_g[l][None], lam_init, n_ctx, n_pad, tq)
        hl = _rglru(xr, conv_w[l], conv_b[l][None], w_gate[l], b_gate[l], lru_lam[l][:, None, :],
                    n_ctx_chunks)
        ys = _s5(us, bd[l], cd[l], ar[l], ai[l], n_ctx_chunks)
        xs = _merge(xs, mod[l], ya, hl, ys, us, za, zr, zs, gl, s5_d[l][None], w_glu_h[l],
                    b_glu_h[l][None], w_branch_h[l], w_out_b[l], n_ctx_chunks, n_pad_chunks,
                    final_g=final_g[None] if l == depth - 1 else None)
    return xs
```
